```python
import math
import jax, jax.numpy as jnp
from jax import lax
import numpy as np

D_MODEL = 1024
BATCH = 8
SEQ = 4096
DEPTH = 4

MEM_LEN = 256
CONV_WIDTH = D_MODEL // 2
CONV_KERNEL = 31
SSM_WIDTH = D_MODEL // 2
SSM_GROUP = 16
SSM_GROUPS = SSM_WIDTH // SSM_GROUP
SSM_STATE = 64
XATTN_HEADS = 4
XATTN_HEAD_DIM = D_MODEL // XATTN_HEADS
D_FF = 2816
N_EXPERTS = 8
TOP_K = 2
D_FF_EXPERT = 3584
N_MOE = DEPTH // 2
N_DENSE = DEPTH - N_MOE
IN_COLS = 2 * CONV_WIDTH + SSM_WIDTH + 2 * D_MODEL
RMS_EPS = 1e-6
LN_EPS = 1e-5
DT_MIN = 1e-3
DT_MAX = 1e-1

kernel_name = "hybrid_conv_s5_xattn_moe_trunk"


def rmsnorm(x, g):
    xf = x.astype(jnp.float32)
    y = xf * lax.rsqrt(jnp.mean(xf * xf, axis=-1, keepdims=True) + RMS_EPS)
    return (y * g.astype(jnp.float32)).astype(x.dtype)


def layernorm(x, g, b):
    xf = x.astype(jnp.float32)
    mu = jnp.mean(xf, axis=-1, keepdims=True)
    xc = xf - mu
    y = xc * lax.rsqrt(jnp.mean(xc * xc, axis=-1, keepdims=True) + LN_EPS)
    return (y * g.astype(jnp.float32) + b.astype(jnp.float32)).astype(x.dtype)


def causal_depthwise_conv(u, w, b):
    c = u.shape[-1]
    y = lax.conv_general_dilated(
        u, w[:, None, :].astype(u.dtype), window_strides=(1,),
        padding=[(CONV_KERNEL - 1, 0)],
        dimension_numbers=("NWC", "WIO", "NWC"),
        feature_group_count=c)
    return y + b.astype(u.dtype)


def conv_branch(u_glu, conv_w, conv_b, ln_g, ln_b, w_pw):
    a, gate = jnp.split(u_glu, 2, axis=-1)
    u = a * jax.nn.sigmoid(gate)
    u = causal_depthwise_conv(u, conv_w, conv_b)
    u = jax.nn.silu(layernorm(u, ln_g, ln_b))
    return u @ w_pw


def ssm_branch(u, lam_re, lam_im, log_dt, b_re, b_im, c_re, c_im, d_skip, w_glu):
    dtype = u.dtype
    bsz, s, _ = u.shape
    uf = u.astype(jnp.float32).reshape(bsz, s, SSM_GROUPS, SSM_GROUP)
    lr = lam_re.astype(jnp.float32)
    li = lam_im.astype(jnp.float32)
    dt = jnp.exp(log_dt.astype(jnp.float32))[:, None]
    mag = jnp.exp(lr * dt)
    ar = mag * jnp.cos(li * dt)
    ai = mag * jnp.sin(li * dt)
    den = lr * lr + li * li
    xr = ar - 1.0
    kr = (xr * lr + ai * li) / den
    ki = (ai * lr - xr * li) / den
    br = b_re.astype(jnp.float32)
    bi = b_im.astype(jnp.float32)
    bbar_r = kr[..., None] * br - ki[..., None] * bi
    bbar_i = kr[..., None] * bi + ki[..., None] * br
    bu_r = jnp.einsum("bsgh,gph->bsgp", uf, bbar_r)
    bu_i = jnp.einsum("bsgh,gph->bsgp", uf, bbar_i)
    a_r = jnp.broadcast_to(ar[None, None], (1, s, SSM_GROUPS, SSM_STATE))
    a_i = jnp.broadcast_to(ai[None, None], (1, s, SSM_GROUPS, SSM_STATE))

    def combine(e1, e2):
        a1r, a1i, b1r, b1i = e1
        a2r, a2i, b2r, b2i = e2
        return (a2r * a1r - a2i * a1i,
                a2r * a1i + a2i * a1r,
                a2r * b1r - a2i * b1i + b2r,
                a2r * b1i + a2i * b1r + b2i)

    _, _, h_r, h_i = lax.associative_scan(combine, (a_r, a_i, bu_r, bu_i), axis=1)
    y = (jnp.einsum("bsgp,ghp->bsgh", h_r, c_re.astype(jnp.float32))
         - jnp.einsum("bsgp,ghp->bsgh", h_i, c_im.astype(jnp.float32))
         + d_skip.astype(jnp.float32) * uf)
    y = jax.nn.gelu(y).reshape(bsz, s, SSM_WIDTH).astype(dtype)
    za, zg = jnp.split(y @ w_glu, 2, axis=-1)
    return za * jax.nn.sigmoid(zg)


def cross_attention(hn, mem_n, w_q, w_kv, w_o):
    bsz, s, _ = hn.shape
    m = mem_n.shape[1]
    q = (hn @ w_q).reshape(bsz, s, XATTN_HEADS, XATTN_HEAD_DIM)
    k, v = jnp.split(mem_n @ w_kv, 2, axis=-1)
    k = k.reshape(bsz, m, XATTN_HEADS, XATTN_HEAD_DIM)
    v = v.reshape(bsz, m, XATTN_HEADS, XATTN_HEAD_DIM)
    scores = jnp.einsum("bshd,bmhd->bhsm", q, k).astype(jnp.float32) / math.sqrt(XATTN_HEAD_DIM)
    p = jax.nn.softmax(scores, axis=-1).astype(v.dtype)
    o = jnp.einsum("bhsm,bmhd->bshd", p, v).reshape(bsz, s, D_MODEL)
    return o @ w_o


def swiglu(x, w_gate_up, w_down):
    g, u = jnp.split(x @ w_gate_up, 2, axis=-1)
    return (jax.nn.silu(g) * u) @ w_down


def moe_swiglu(x, router_w, router_b, w_gate_up, w_down):
    bsz, s, d = x.shape
    t = x.reshape(bsz * s, d)
    logits = (t @ router_w).astype(jnp.float32) + router_b.astype(jnp.float32)
    top_vals, top_idx = lax.top_k(logits, TOP_K)
    top_w = jax.nn.softmax(top_vals, axis=-1)
    gates = jnp.sum(jax.nn.one_hot(top_idx, N_EXPERTS, dtype=jnp.float32) * top_w[..., None], axis=1)
    out = jnp.zeros((bsz * s, d), jnp.float32)
    for e in range(N_EXPERTS):
        out = out + gates[:, e:e + 1] * swiglu(t, w_gate_up[e], w_down[e]).astype(jnp.float32)
    return out.astype(x.dtype).reshape(bsz, s, d)


def setup_inputs(seed: int = 0) -> dict:
    key = jax.random.key(seed)
    ks = iter(jax.random.split(key, 40))
    f32 = jnp.float32
    L = DEPTH

    def nrm(shape, scale):
        return jax.random.normal(next(ks), shape, f32) * scale

    def gain(shape):
        return 1.0 + 0.02 * jax.random.normal(next(ks), shape, f32)

    lam_im0 = jnp.pi * jnp.arange(SSM_STATE, dtype=f32)
    return {
        "x": nrm((BATCH, SEQ, D_MODEL), 1.0),
        "mem": nrm((BATCH, MEM_LEN, D_MODEL), 1.0),
        "mem_norm_g": gain((D_MODEL,)),
        "norm_mix_g": gain((L, D_MODEL)),
        "w_in": nrm((L, D_MODEL, IN_COLS), D_MODEL ** -0.5),
        "conv_w": nrm((L, CONV_KERNEL, CONV_WIDTH), CONV_KERNEL ** -0.5),
        "conv_b": nrm((L, CONV_WIDTH), 0.02),
        "conv_ln_g": gain((L, CONV_WIDTH)),
        "conv_ln_b": nrm((L, CONV_WIDTH), 0.02),
        "w_conv_pw": nrm((L, CONV_WIDTH, D_MODEL), CONV_WIDTH ** -0.5),
        "ssm_lambda_re": -0.5 + nrm((L, SSM_GROUPS, SSM_STATE), 0.01),
        "ssm_lambda_im": lam_im0 + nrm((L, SSM_GROUPS, SSM_STATE), 0.01),
        "ssm_log_dt": jax.random.uniform(next(ks), (L, SSM_GROUPS), f32,
                                         math.log(DT_MIN), math.log(DT_MAX)),
        "ssm_b_re": nrm((L, SSM_GROUPS, SSM_STATE, SSM_GROUP), (2 * SSM_GROUP) ** -0.5),
        "ssm_b_im": nrm((L, SSM_GROUPS, SSM_STATE, SSM_GROUP), (2 * SSM_GROUP) ** -0.5),
        "ssm_c_re": nrm((L, SSM_GROUPS, SSM_GROUP, SSM_STATE), SSM_STATE ** -0.5),
        "ssm_c_im": nrm((L, SSM_GROUPS, SSM_GROUP, SSM_STATE), SSM_STATE ** -0.5),
        "ssm_d": nrm((L, SSM_GROUPS, SSM_GROUP), 1.0),
        "w_ssm_glu": nrm((L, SSM_WIDTH, 2 * D_MODEL), SSM_WIDTH ** -0.5),
        "w_out": nrm((L, D_MODEL, D_MODEL), D_MODEL ** -0.5),
        "norm_xattn_g": gain((L, D_MODEL)),
        "w_q": nrm((L, D_MODEL, D_MODEL), D_MODEL ** -0.5),
        "w_kv": nrm((L, D_MODEL, 2 * D_MODEL), D_MODEL ** -0.5),
        "w_o": nrm((L, D_MODEL, D_MODEL), D_MODEL ** -0.5),
        "norm_ffn_g": gain((L, D_MODEL)),
        "ffn_w_gate_up": nrm((N_DENSE, D_MODEL, 2 * D_FF), D_MODEL ** -0.5),
        "ffn_w_down": nrm((N_DENSE, D_FF, D_MODEL), D_FF ** -0.5),
        "router_w": nrm((N_MOE, D_MODEL, N_EXPERTS), D_MODEL ** -0.5),
        "router_b": nrm((N_MOE, N_EXPERTS), 0.01),
        "moe_w_gate_up": nrm((N_MOE, N_EXPERTS, D_MODEL, 2 * D_FF_EXPERT), D_MODEL ** -0.5),
        "moe_w_down": nrm((N_MOE, N_EXPERTS, D_FF_EXPERT, D_MODEL), D_FF_EXPERT ** -0.5),
        "final_norm_g": gain((D_MODEL,)),
    }


def reference(x, mem, mem_norm_g, norm_mix_g, w_in, conv_w, conv_b, conv_ln_g, conv_ln_b,
              w_conv_pw, ssm_lambda_re, ssm_lambda_im, ssm_log_dt, ssm_b_re, ssm_b_im,
              ssm_c_re, ssm_c_im, ssm_d, w_ssm_glu, w_out, norm_xattn_g, w_q, w_kv, w_o,
              norm_ffn_g, ffn_w_gate_up, ffn_w_down, router_w, router_b, moe_w_gate_up,
              moe_w_down, final_norm_g):
    mem_n = rmsnorm(mem, mem_norm_g)
    c0 = 2 * CONV_WIDTH
    c1 = c0 + SSM_WIDTH
    h = x
    for i in range(DEPTH):
        hn = rmsnorm(h, norm_mix_g[i])
        proj = hn @ w_in[i]
        u_conv = proj[..., :c0]
        u_ssm = proj[..., c0:c1]
        g_a, g_b = jnp.split(jax.nn.sigmoid(proj[..., c1:]), 2, axis=-1)
        y_a = conv_branch(u_conv, conv_w[i], conv_b[i], conv_ln_g[i], conv_ln_b[i], w_conv_pw[i])
        y_b = ssm_branch(u_ssm, ssm_lambda_re[i], ssm_lambda_im[i], ssm_log_dt[i],
                         ssm_b_re[i], ssm_b_im[i], ssm_c_re[i], ssm_c_im[i], ssm_d[i],
                         w_ssm_glu[i])
        h = h + (g_a * y_a + g_b * y_b) @ w_out[i]
        h = h + cross_attention(rmsnorm(h, norm_xattn_g[i]), mem_n, w_q[i], w_kv[i], w_o[i])
        hn = rmsnorm(h, norm_ffn_g[i])
        j = i // 2
        if i % 2 == 0:
            h = h + swiglu(hn, ffn_w_gate_up[j], ffn_w_down[j])
        else:
            h = h + moe_swiglu(hn, router_w[j], router_b[j], moe_w_gate_up[j], moe_w_down[j])
    return rmsnorm(h, final_norm_g)
```

```python
import functools
import math

import jax
import jax.numpy as jnp
from jax import lax
from jax.experimental import pallas as pl
from jax.experimental.pallas import tpu as pltpu

F32 = jnp.float32
BF16 = jnp.bfloat16

RMS_EPS = 1e-6
LN_EPS = 1e-5
CONV_KERNEL = 31
SSM_GROUP = 16
SSM_STATE = 64
XATTN_HEADS = 4
N_EXPERTS = 8
TOP_K = 2

SUBLANES = 8
LANES = 128
HALO_BLOCK = 256
VMEM_LIMIT = 52 * 2 ** 20


def _params(*sem):
    return pltpu.CompilerParams(dimension_semantics=sem, vmem_limit_bytes=VMEM_LIMIT)


def _rms(x, g):
    return x * lax.rsqrt(jnp.mean(x * x, axis=-1, keepdims=True) + RMS_EPS) * g


def _dot(a, b):
    return jnp.dot(a, b, preferred_element_type=F32)


def _in_kernel(h_ref, g_ref, w_ref, uc_ref, us_ref, gt_ref, *, cw, sw):
    xn = _rms(h_ref[...], g_ref[...]).astype(BF16)
    a = _dot(xn, w_ref[:, 0:cw])
    gate = _dot(xn, w_ref[:, cw:2 * cw])
    uc_ref[...] = a * jax.nn.sigmoid(gate)
    us_ref[...] = _dot(xn, w_ref[:, 2 * cw:2 * cw + sw])
    gt_ref[...] = jax.nn.sigmoid(_dot(xn, w_ref[:, 2 * cw + sw:])).astype(BF16)


def _in_proj(h, g, w, layer, cw, sw, tm):
    t, d = h.shape
    cols = w.shape[-1]
    ng = cols - 2 * cw - sw
    return pl.pallas_call(
        functools.partial(_in_kernel, cw=cw, sw=sw),
        grid=(t // tm,),
        in_specs=[pl.BlockSpec((tm, d), lambda i: (i, 0)),
                  pl.BlockSpec((None, 1, d), lambda i: (layer, 0, 0)),
                  pl.BlockSpec((None, d, cols), lambda i: (layer, 0, 0))],
        out_specs=[pl.BlockSpec((tm, cw), lambda i: (i, 0)),
                   pl.BlockSpec((tm, sw), lambda i: (i, 0)),
                   pl.BlockSpec((tm, ng), lambda i: (i, 0))],
        out_shape=[jax.ShapeDtypeStruct((t, cw), F32),
                   jax.ShapeDtypeStruct((t, sw), F32),
                   jax.ShapeDtypeStruct((t, ng), BF16)],
        compiler_params=_params("parallel"),
        name="in_proj",
    )(h, g, w)


def _conv_kernel(cur_ref, prev_ref, cw_ref, cb_ref, lg_ref, lb_ref, wpw_ref, out_ref,
                 ext_ref, act_ref, *, rows_per_chunk):
    i = pl.program_id(0)
    tm, c = cur_ref.shape
    ext_ref[0:HALO_BLOCK, :] = jnp.where(i > 0, prev_ref[...], 0.0)
    ext_ref[HALO_BLOCK:, :] = cur_ref[...]
    first = HALO_BLOCK - (CONV_KERNEL - 1) * SUBLANES
    rc = rows_per_chunk

    def chunk(ci, carry):
        r0 = pl.multiple_of(ci * rc, rc)
        acc = jnp.broadcast_to(cb_ref[...], (rc, c))
        for k in range(CONV_KERNEL):
            acc = acc + cw_ref[k:k + 1, :] * ext_ref[pl.ds(r0 + first + SUBLANES * k, rc), :]
        mu = jnp.mean(acc, axis=-1, keepdims=True)
        xc = acc - mu
        y = xc * lax.rsqrt(jnp.mean(xc * xc, axis=-1, keepdims=True) + LN_EPS)
        y = y * lg_ref[...] + lb_ref[...]
        act_ref[pl.ds(r0, rc), :] = jax.nn.silu(y).astype(BF16)
        return carry

    lax.fori_loop(0, tm // rc, chunk, 0)
    out_ref[...] = _dot(act_ref[...], wpw_ref[...]).astype(BF16)


def _conv_branch(u, conv_w, conv_b, ln_g, ln_b, w_pw, layer, tm):
    t, c = u.shape
    d = w_pw.shape[-1]
    ratio = tm // HALO_BLOCK
    return pl.pallas_call(
        functools.partial(_conv_kernel, rows_per_chunk=32),
        grid=(t // tm,),
        in_specs=[pl.BlockSpec((tm, c), lambda i: (i, 0)),
                  pl.BlockSpec((HALO_BLOCK, c), lambda i: (jnp.maximum(i * ratio - 1, 0), 0)),
                  pl.BlockSpec((None, CONV_KERNEL, c), lambda i: (layer, 0, 0)),
                  pl.BlockSpec((None, 1, c), lambda i: (layer, 0, 0)),
                  pl.BlockSpec((None, 1, c), lambda i: (layer, 0, 0)),
                  pl.BlockSpec((None, 1, c), lambda i: (layer, 0, 0)),
                  pl.BlockSpec((None, c, d), lambda i: (layer, 0, 0))],
        out_specs=pl.BlockSpec((tm, d), lambda i: (i, 0)),
        out_shape=jax.ShapeDtypeStruct((t, d), BF16),
        scratch_shapes=[pltpu.VMEM((tm + HALO_BLOCK, c), F32), pltpu.VMEM((tm, c), BF16)],
        compiler_params=_params("parallel"),
        name="conv_branch",
    )(u, u, conv_w, conv_b, ln_g, ln_b, w_pw)


def _ssm_kernel(u_ref, bm_ref, cm_ref, a_ref, d_ref, out_ref, hs_ref, st_ref, *, chains):
    step_idx = pl.program_id(0)
    rows, width = u_ref.shape
    hw = width // 2
    hs_cols = hs_ref.shape[1] // 2
    hstates = hs_cols // 2

    @pl.when(step_idx == 0)
    def _():
        st_ref[...] = jnp.zeros_like(st_ref)

    for hf in range(2):
        ub = u_ref[:, hf * hw:(hf + 1) * hw].astype(BF16)
        hs_ref[:, hf * hs_cols:(hf + 1) * hs_cols] = _dot(ub, bm_ref[hf])

    blocks = [(hf, j) for hf in range(2) for j in range(hstates // LANES)]
    for g0 in range(0, len(blocks), chains):
        cols = [(hf * hs_cols + j * LANES, hf * hs_cols + hstates + j * LANES, hf * hstates + j * LANES)
                for hf, j in blocks[g0:g0 + chains]]
        ars = [a_ref[0:SUBLANES, ac:ac + LANES] for _, _, ac in cols]
        ais = [a_ref[SUBLANES:2 * SUBLANES, ac:ac + LANES] for _, _, ac in cols]
        init = tuple((st_ref[:, cr:cr + LANES], st_ref[:, ci:ci + LANES]) for cr, ci, _ in cols)

        def step(t, carry, cols=cols, ars=ars, ais=ais):
            r0 = pl.multiple_of(t * SUBLANES, SUBLANES)
            new = []
            for n, (cr, ci, _) in enumerate(cols):
                hr, hi = carry[n]
                bur = hs_ref[pl.ds(r0, SUBLANES), cr:cr + LANES]
                bui = hs_ref[pl.ds(r0, SUBLANES), ci:ci + LANES]
                nr = ars[n] * hr - ais[n] * hi + bur
                ni = ars[n] * hi + ais[n] * hr + bui
                hs_ref[pl.ds(r0, SUBLANES), cr:cr + LANES] = nr
                hs_ref[pl.ds(r0, SUBLANES), ci:ci + LANES] = ni
                new.append((nr, ni))
            return tuple(new)

        fin = lax.fori_loop(0, rows // SUBLANES, step, init, unroll=8)
        for n, (cr, ci, _) in enumerate(cols):
            st_ref[:, cr:cr + LANES] = fin[n][0]
            st_ref[:, ci:ci + LANES] = fin[n][1]

    for hf in range(2):
        hb = hs_ref[:, hf * hs_cols:(hf + 1) * hs_cols].astype(BF16)
        y = _dot(hb, cm_ref[hf]) + d_ref[:, hf * hw:(hf + 1) * hw] * u_ref[:, hf * hw:(hf + 1) * hw]
        out_ref[:, hf * hw:(hf + 1) * hw] = jax.nn.gelu(y).astype(BF16)


def _ssm_branch(u, bm, cm, a_rows, d_row, rows):
    t, width = u.shape
    ncols = 2 * bm.shape[-1]
    return pl.pallas_call(
        functools.partial(_ssm_kernel, chains=4),
        grid=(t // rows,),
        in_specs=[pl.BlockSpec((rows, width), lambda i: (i, 0)),
                  pl.BlockSpec(bm.shape, lambda i: (0, 0, 0)),
                  pl.BlockSpec(cm.shape, lambda i: (0, 0, 0)),
                  pl.BlockSpec(a_rows.shape, lambda i: (0, 0)),
                  pl.BlockSpec(d_row.shape, lambda i: (0, 0))],
        out_specs=pl.BlockSpec((rows, width), lambda i: (i, 0)),
        out_shape=jax.ShapeDtypeStruct((t, width), BF16),
        scratch_shapes=[pltpu.VMEM((rows, ncols), F32), pltpu.VMEM((SUBLANES, ncols), F32)],
        compiler_params=_params("arbitrary"),
        name="ssm_branch",
    )(u, bm, cm, a_rows, d_row)


def _ssm_tables(lam_re, lam_im, log_dt, b_re, b_im, c_re, c_im, d_skip):
    g, p = lam_re.shape
    gh = g // 2
    dt = jnp.exp(log_dt)[:, None]
    mag = jnp.exp(lam_re * dt)
    ar = mag * jnp.cos(lam_im * dt)
    ai = mag * jnp.sin(lam_im * dt)
    den = lam_re * lam_re + lam_im * lam_im
    xr = ar - 1.0
    kr = (xr * lam_re + ai * lam_im) / den
    ki = (ai * lam_re - xr * lam_im) / den
    bbar_r = kr[..., None] * b_re - ki[..., None] * b_im
    bbar_i = kr[..., None] * b_im + ki[..., None] * b_re
    eye = jnp.eye(gh, dtype=F32)

    def bd_in(m):
        m = m.reshape(2, gh, p, SSM_GROUP)
        return jnp.einsum("aqph,qr->aqhrp", m, eye).reshape(2, gh * SSM_GROUP, gh * p)

    def bd_out(m):
        m = m.reshape(2, gh, SSM_GROUP, p)
        return jnp.einsum("aqhp,qr->aqprh", m, eye).reshape(2, gh * p, gh * SSM_GROUP)

    bm = jnp.concatenate([bd_in(bbar_r), bd_in(bbar_i)], axis=-1).astype(BF16)
    cm = jnp.concatenate([bd_out(c_re), -bd_out(c_im)], axis=1).astype(BF16)
    a_rows = jnp.concatenate([jnp.broadcast_to(ar.reshape(1, g * p), (SUBLANES, g * p)),
                              jnp.broadcast_to(ai.reshape(1, g * p), (SUBLANES, g * p))], axis=0)
    return bm, cm, a_rows, d_skip.reshape(1, g * SSM_GROUP)


def _merge_kernel(ys_ref, wglu_ref, gt_ref, ya_ref, h_ref, wout_ref, gx_ref, wq_ref, hout_ref, q_ref):
    d = h_ref.shape[1]
    z = _dot(ys_ref[...], wglu_ref[...])
    yb = z[:, :d] * jax.nn.sigmoid(z[:, d:])
    m = gt_ref[:, :d].astype(F32) * ya_ref[...].astype(F32) + gt_ref[:, d:].astype(F32) * yb
    h2 = h_ref[...] + _dot(m.astype(BF16), wout_ref[...])
    hout_ref[...] = h2
    q_ref[...] = _dot(_rms(h2, gx_ref[...]).astype(BF16), wq_ref[...]).astype(BF16)


def _merge(ys, w_glu, gates, ya, h, w_out, gx, w_q, layer, tm):
    t, d = h.shape
    sw = ys.shape[1]
    return pl.pallas_call(
        _merge_kernel,
        grid=(t // tm,),
        in_specs=[pl.BlockSpec((tm, sw), lambda i: (i, 0)),
                  pl.BlockSpec((None, sw, 2 * d), lambda i: (layer, 0, 0)),
                  pl.BlockSpec((tm, 2 * d), lambda i: (i, 0)),
                  pl.BlockSpec((tm, d), lambda i: (i, 0)),
                  pl.BlockSpec((tm, d), lambda i: (i, 0)),
                  pl.BlockSpec((None, d, d), lambda i: (layer, 0, 0)),
                  pl.BlockSpec((None, 1, d), lambda i: (layer, 0, 0)),
                  pl.BlockSpec((None, d, d), lambda i: (layer, 0, 0))],
        out_specs=[pl.BlockSpec((tm, d), lambda i: (i, 0)),
                   pl.BlockSpec((tm, d), lambda i: (i, 0))],
        out_shape=[jax.ShapeDtypeStruct((t, d), F32), jax.ShapeDtypeStruct((t, d), BF16)],
        compiler_params=_params("parallel"),
        name="merge",
    )(ys, w_glu, gates, ya, h, w_out, gx, w_q)


def _kv_kernel(mem_ref, g_ref, w_ref, out_ref):
    xn = _rms(mem_ref[...], g_ref[...]).astype(BF16)
    out_ref[...] = _dot(xn, w_ref[...]).astype(BF16)


def _kv_proj(mem2d, g, w_kv):
    nl, d, d2 = w_kv.shape
    bm = mem2d.shape[0]
    return pl.pallas_call(
        _kv_kernel,
        grid=(nl, d2 // d),
        in_specs=[pl.BlockSpec((bm, d), lambda l, j: (0, 0)),
                  pl.BlockSpec((1, d), lambda l, j: (0, 0)),
                  pl.BlockSpec((None, d, d), lambda l, j: (l, 0, j))],
        out_specs=pl.BlockSpec((None, bm, d), lambda l, j: (l, 0, j)),
        out_shape=jax.ShapeDtypeStruct((nl, bm, d2), BF16),
        compiler_params=_params("parallel", "parallel"),
        name="kv_proj",
    )(mem2d, g, w_kv)


def _attn_kernel(q_ref, kv_ref, h_ref, wo_ref, out_ref):
    d = h_ref.shape[1]
    hd = d // XATTN_HEADS
    scale = 1.0 / math.sqrt(hd)
    acc = h_ref[...]
    for n in range(XATTN_HEADS):
        qh = q_ref[:, n * hd:(n + 1) * hd]
        kh = kv_ref[:, n * hd:(n + 1) * hd]
        vh = kv_ref[:, d + n * hd:d + (n + 1) * hd]
        s = lax.dot_general(qh, kh, (((1,), (1,)), ((), ())), preferred_element_type=F32) * scale
        e = jnp.exp(s - jnp.max(s, axis=-1, keepdims=True))
        p = e / jnp.sum(e, axis=-1, keepdims=True)
        oh = _dot(p.astype(BF16), vh)
        acc = acc + _dot(oh.astype(BF16), wo_ref[n * hd:(n + 1) * hd, :])
    out_ref[...] = acc


def _attention(q, kv, h, w_o, layer, nb, ts):
    s, bd = h.shape
    d = bd // nb
    m = kv.shape[1] // nb
    return pl.pallas_call(
        _attn_kernel,
        grid=(nb, s // ts),
        in_specs=[pl.BlockSpec((ts, d), lambda b, i: (i, b)),
                  pl.BlockSpec((None, m, 2 * d), lambda b, i: (layer, b, 0)),
                  pl.BlockSpec((ts, d), lambda b, i: (i, b)),
                  pl.BlockSpec((None, d, d), lambda b, i: (layer, 0, 0))],
        out_specs=pl.BlockSpec((ts, d), lambda b, i: (i, b)),
        out_shape=jax.ShapeDtypeStruct((s, bd), F32),
        compiler_params=_params("parallel", "parallel"),
        name="xattn",
    )(q, kv, h, w_o)


def _swiglu_step(xn, wg_ref, wu_ref, wd_ref):
    g = _dot(xn, wg_ref[...])
    u = _dot(xn, wu_ref[...])
    return _dot((jax.nn.silu(g) * u).astype(BF16), wd_ref[...])


def _ffn_kernel(h_ref, g_ref, wg_ref, wu_ref, wd_ref, out_ref, xn_ref, acc_ref):
    f = pl.program_id(1)

    @pl.when(f == 0)
    def _():
        xn_ref[...] = _rms(h_ref[...], g_ref[...]).astype(BF16)
        acc_ref[...] = jnp.zeros_like(acc_ref)

    acc_ref[...] += _swiglu_step(xn_ref[...], wg_ref, wu_ref, wd_ref)

    @pl.when(f == pl.num_programs(1) - 1)
    def _():
        out_ref[...] = h_ref[...] + acc_ref[...]


def _ff_tile(ff):
    for tf in (512, 256, 128):
        if ff % tf == 0:
            return tf
    raise ValueError(f"hidden size {ff} is not a multiple of {LANES}")


def _dense_ffn(h, g, w_gu, w_d, layer, j, tm):
    t, d = h.shape
    ff = w_d.shape[1]
    tf = _ff_tile(ff)
    nf = ff // tf
    return pl.pallas_call(
        _ffn_kernel,
        grid=(t // tm, nf),
        in_specs=[pl.BlockSpec((tm, d), lambda i, f: (i, 0)),
                  pl.BlockSpec((None, 1, d), lambda i, f: (layer, 0, 0)),
                  pl.BlockSpec((None, d, tf), lambda i, f: (j, 0, f)),
                  pl.BlockSpec((None, d, tf), lambda i, f: (j, 0, nf + f)),
                  pl.BlockSpec((None, tf, d), lambda i, f: (j, f, 0))],
        out_specs=pl.BlockSpec((tm, d), lambda i, f: (i, 0)),
        out_shape=jax.ShapeDtypeStruct((t, d), F32),
        scratch_shapes=[pltpu.VMEM((tm, d), BF16), pltpu.VMEM((tm, d), F32)],
        compiler_params=_params("parallel", "arbitrary"),
        name="dense_ffn",
    )(h, g, w_gu, w_gu, w_d)


def _router_kernel(h_ref, g_ref, rw_ref, rb_ref, hn_ref, idx_ref, wt_ref):
    xn = _rms(h_ref[...], g_ref[...])
    hn_ref[...] = xn.astype(BF16)
    logits = jnp.dot(xn, rw_ref[...], preferred_element_type=F32,
                     precision=lax.Precision.HIGHEST) + rb_ref[...]
    lane = lax.broadcasted_iota(jnp.int32, logits.shape, 1)
    lanef = lane.astype(F32)
    neg = jnp.float32(-jnp.inf)
    big = jnp.float32(LANES)
    l1 = jnp.where(lane < N_EXPERTS, logits, neg)
    m1 = jnp.max(l1, axis=-1, keepdims=True)
    i1 = jnp.min(jnp.where(l1 == m1, lanef, big), axis=-1, keepdims=True)
    l2 = jnp.where(lanef == i1, neg, l1)
    m2 = jnp.max(l2, axis=-1, keepdims=True)
    i2 = jnp.min(jnp.where(l2 == m2, lanef, big), axis=-1, keepdims=True)
    e = jnp.exp(m2 - m1)
    w1 = 1.0 / (1.0 + e)
    w2 = e / (1.0 + e)
    idx_ref[...] = jnp.where(lane == 0, i1, jnp.where(lane == 1, i2, 0.0)).astype(jnp.int32)
    wt_ref[...] = jnp.where(lane == 0, w1, jnp.where(lane == 1, w2, 0.0))


def _router(h, g, rw, rb, layer, tm):
    t, d = h.shape
    return pl.pallas_call(
        _router_kernel,
        grid=(t // tm,),
        in_specs=[pl.BlockSpec((tm, d), lambda i: (i, 0)),
                  pl.BlockSpec((None, 1, d), lambda i: (layer, 0, 0)),
                  pl.BlockSpec((d, LANES), lambda i: (0, 0)),
                  pl.BlockSpec((1, LANES), lambda i: (0, 0))],
        out_specs=[pl.BlockSpec((tm, d), lambda i: (i, 0)),
                   pl.BlockSpec((tm, LANES), lambda i: (i, 0)),
                   pl.BlockSpec((tm, LANES), lambda i: (i, 0))],
        out_shape=[jax.ShapeDtypeStruct((t, d), BF16),
                   jax.ShapeDtypeStruct((t, LANES), jnp.int32),
                   jax.ShapeDtypeStruct((t, LANES), F32)],
        compiler_params=_params("parallel"),
        name="router",
    )(h, g, rw, rb)


def _moe_kernel(te_ref, tv_ref, x_ref, wg_ref, wu_ref, wd_ref, out_ref, acc_ref):
    i = pl.program_id(0)
    f = pl.program_id(1)
    last = pl.num_programs(1) - 1
    valid = tv_ref[i] > 0

    @pl.when(f == 0)
    def _():
        acc_ref[...] = jnp.zeros_like(acc_ref)

    @pl.when(valid)
    def _():
        acc_ref[...] += _swiglu_step(x_ref[...], wg_ref, wu_ref, wd_ref)

    @pl.when(f == last)
    def _():
        out_ref[...] = acc_ref[...]


def _moe_experts(xg, w_gu, w_d, tile_expert, tile_valid, j, tm):
    r, d = xg.shape
    ff = w_d.shape[2]
    tf = _ff_tile(ff)
    nf = ff // tf
    grid_spec = pltpu.PrefetchScalarGridSpec(
        num_scalar_prefetch=2,
        grid=(r // tm, nf),
        in_specs=[pl.BlockSpec((tm, d), lambda i, f, te, tv: (i, 0)),
                  pl.BlockSpec((None, None, d, tf), lambda i, f, te, tv: (j, te[i], 0, f)),
                  pl.BlockSpec((None, None, d, tf), lambda i, f, te, tv: (j, te[i], 0, nf + f)),
                  pl.BlockSpec((None, None, tf, d), lambda i, f, te, tv: (j, te[i], f, 0))],
        out_specs=pl.BlockSpec((tm, d), lambda i, f, te, tv: (i, 0)),
        scratch_shapes=[pltpu.VMEM((tm, d), F32)])
    return pl.pallas_call(
        _moe_kernel,
        grid_spec=grid_spec,
        out_shape=jax.ShapeDtypeStruct((r, d), F32),
        compiler_params=_params("parallel", "arbitrary"),
        name="moe_experts",
    )(tile_expert, tile_valid, xg, w_gu, w_gu, w_d)


def _dispatch_tables(idx, tm):
    t = idx.shape[0]
    npairs = t * TOP_K
    ntiles = npairs // tm + N_EXPERTS
    e_flat = idx.reshape(npairs)
    onehot = (e_flat[:, None] == jnp.arange(N_EXPERTS, dtype=jnp.int32)[None, :]).astype(jnp.int32)
    csum = jnp.cumsum(onehot, axis=0)
    counts = csum[-1]
    rank = jnp.take_along_axis(csum, e_flat[:, None], axis=1)[:, 0] - 1
    padded = ((counts + tm - 1) // tm) * tm
    pend = jnp.cumsum(padded)
    pstart = pend - padded
    cstart = jnp.cumsum(counts) - counts
    pos = pstart[e_flat] + rank
    order = jnp.argsort(e_flat * npairs + jnp.arange(npairs, dtype=jnp.int32))
    tile_start = jnp.arange(ntiles, dtype=jnp.int32) * tm
    tile_expert = jnp.minimum(jnp.searchsorted(pend, tile_start, side="right"),
                              N_EXPERTS - 1).astype(jnp.int32)
    tile_valid = (tile_start < pend[-1]).astype(jnp.int32)
    slot = jnp.arange(ntiles * tm, dtype=jnp.int32)
    slot_e = jnp.repeat(tile_expert, tm)
    within = slot - pstart[slot_e]
    src = jnp.clip(cstart[slot_e] + within, 0, npairs - 1)
    row_token = jnp.where(within < counts[slot_e], order[src] // TOP_K, 0).astype(jnp.int32)
    return row_token, pos.reshape(t, TOP_K), tile_expert, tile_valid


def _final_kernel(h_ref, g_ref, out_ref):
    out_ref[...] = _rms(h_ref[...], g_ref[...])


def _final_norm(h, g, tm):
    t, d = h.shape
    return pl.pallas_call(
        _final_kernel,
        grid=(t // tm,),
        in_specs=[pl.BlockSpec((tm, d), lambda i: (i, 0)), pl.BlockSpec((1, d), lambda i: (0, 0))],
        out_specs=pl.BlockSpec((tm, d), lambda i: (i, 0)),
        out_shape=jax.ShapeDtypeStruct((t, d), F32),
        compiler_params=_params("parallel"),
        name="final_norm",
    )(h, g)


def kernel(x, mem, mem_norm_g, norm_mix_g, w_in, conv_w, conv_b, conv_ln_g, conv_ln_b, w_conv_pw,
           ssm_lambda_re, ssm_lambda_im, ssm_log_dt, ssm_b_re, ssm_b_im, ssm_c_re, ssm_c_im, ssm_d,
           w_ssm_glu, w_out, norm_xattn_g, w_q, w_kv, w_o, norm_ffn_g, ffn_w_gate_up, ffn_w_down,
           router_w, router_b, moe_w_gate_up, moe_w_down, final_norm_g):
    nb, s, d = x.shape
    assert nb == SUBLANES, "the row layout puts the batch on the sublane axis"
    depth = w_in.shape[0]
    cw = conv_w.shape[-1]
    sw = ssm_d.shape[1] * ssm_d.shape[2]
    t = nb * s
    tm = min(512, t)
    tm_ff = min(1024, t)
    ts = min(512, s)

    w_in_b = w_in.astype(BF16)
    w_pw_b = w_conv_pw.astype(BF16)
    w_glu_b = w_ssm_glu.astype(BF16)
    w_out_b = w_out.astype(BF16)
    w_q_b = w_q.astype(BF16)
    w_kv_b = w_kv.astype(BF16)
    w_o_b = w_o.astype(BF16)
    ffn_gu_b = ffn_w_gate_up.astype(BF16)
    ffn_d_b = ffn_w_down.astype(BF16)
    moe_gu_b = moe_w_gate_up.astype(BF16)
    moe_d_b = moe_w_down.astype(BF16)
    row3 = lambda a: a.reshape(a.shape[0], 1, a.shape[1])
    g_mix, g_x, g_ffn = row3(norm_mix_g), row3(norm_xattn_g), row3(norm_ffn_g)
    cb3, lg3, lb3 = row3(conv_b), row3(conv_ln_g), row3(conv_ln_b)
    rw_pad = jnp.pad(router_w, ((0, 0), (0, 0), (0, LANES - N_EXPERTS)))
    rb_pad = jnp.pad(router_b, ((0, 0), (0, LANES - N_EXPERTS)))

    kv = _kv_proj(mem.reshape(nb * mem.shape[1], d), mem_norm_g.reshape(1, d), w_kv_b)
    h = x.transpose(1, 0, 2).reshape(t, d)

    for i in range(depth):
        u_conv, u_ssm, gates = _in_proj(h, g_mix, w_in_b, i, cw, sw, tm)
        y_a = _conv_branch(u_conv, conv_w, cb3, lg3, lb3, w_pw_b, i, tm)
        tables = _ssm_tables(ssm_lambda_re[i], ssm_lambda_im[i], ssm_log_dt[i], ssm_b_re[i], ssm_b_im[i],
                             ssm_c_re[i], ssm_c_im[i], ssm_d[i])
        y_s = _ssm_branch(u_ssm, *tables, rows=tm)
        h, q = _merge(y_s, w_glu_b, gates, y_a, h, w_out_b, g_x, w_q_b, i, tm)
        h = _attention(q.reshape(s, nb * d), kv, h.reshape(s, nb * d), w_o_b, i, nb, ts).reshape(t, d)
        j = i // 2
        if i % 2 == 0:
            h = _dense_ffn(h, g_ffn, ffn_gu_b, ffn_d_b, i, j, tm_ff)
        else:
            hn, idx, wts = _router(h, g_ffn, rw_pad[j], rb_pad[j:j + 1], i, tm)
            row_token, pos, tile_expert, tile_valid = _dispatch_tables(idx[:, :TOP_K], tm_ff)
            xg = jnp.take(hn, row_token, axis=0)
            yg = _moe_experts(xg, moe_gu_b, moe_d_b, tile_expert, tile_valid, j, tm_ff)
            h = (h + wts[:, 0:1] * jnp.take(yg, pos[:, 0], axis=0)
                 + wts[:, 1:2] * jnp.take(yg, pos[:, 1], axis=0))

    out = _final_norm(h, final_norm_g.reshape(1, d), tm)
    return out.reshape(s, nb, d).transpose(1, 0, 2)
```

```python
import functools
import math

import jax
import jax.numpy as jnp
from jax import lax
from jax.experimental import pallas as pl
from jax.experimental.pallas import tpu as pltpu

F32 = jnp.float32
BF16 = jnp.bfloat16

RMS_EPS = 1e-6
LN_EPS = 1e-5
CONV_KERNEL = 31
SSM_GROUP = 16
SSM_STATE = 64
XATTN_HEADS = 4
N_EXPERTS = 8
TOP_K = 2

SUBLANES = 8
LANES = 128
HALO_BLOCK = 256
VMEM_LIMIT = 52 * 2 ** 20


def _params(*sem):
    return pltpu.CompilerParams(dimension_semantics=sem, vmem_limit_bytes=VMEM_LIMIT)


def _rms(x, g):
    return x * lax.rsqrt(jnp.mean(x * x, axis=-1, keepdims=True) + RMS_EPS) * g


def _dot(a, b):
    return jnp.dot(a, b, preferred_element_type=F32)


def _in_kernel(h_ref, g_ref, w_ref, uc_ref, us_ref, gt_ref, *, cw, sw):
    xn = _rms(h_ref[...], g_ref[...]).astype(BF16)
    a = _dot(xn, w_ref[:, 0:cw])
    gate = _dot(xn, w_ref[:, cw:2 * cw])
    uc_ref[...] = a * jax.nn.sigmoid(gate)
    us_ref[...] = _dot(xn, w_ref[:, 2 * cw:2 * cw + sw])
    gt_ref[...] = jax.nn.sigmoid(_dot(xn, w_ref[:, 2 * cw + sw:])).astype(BF16)


def _in_proj(h, g, w, layer, cw, sw, tm):
    t, d = h.shape
    cols = w.shape[-1]
    ng = cols - 2 * cw - sw
    return pl.pallas_call(
        functools.partial(_in_kernel, cw=cw, sw=sw),
        grid=(t // tm,),
        in_specs=[pl.BlockSpec((tm, d), lambda i: (i, 0)),
                  pl.BlockSpec((None, 1, d), lambda i: (layer, 0, 0)),
                  pl.BlockSpec((None, d, cols), lambda i: (layer, 0, 0))],
        out_specs=[pl.BlockSpec((tm, cw), lambda i: (i, 0)),
                   pl.BlockSpec((tm, sw), lambda i: (i, 0)),
                   pl.BlockSpec((tm, ng), lambda i: (i, 0))],
        out_shape=[jax.ShapeDtypeStruct((t, cw), F32),
                   jax.ShapeDtypeStruct((t, sw), F32),
                   jax.ShapeDtypeStruct((t, ng), BF16)],
        compiler_params=_params("parallel"),
        name="in_proj",
    )(h, g, w)


def _conv_kernel(cur_ref, prev_ref, cw_ref, cb_ref, lg_ref, lb_ref, wpw_ref, out_ref,
                 ext_ref, act_ref, *, rows_per_chunk):
    i = pl.program_id(0)
    tm, c = cur_ref.shape
    ext_ref[0:HALO_BLOCK, :] = jnp.where(i > 0, prev_ref[...], 0.0)
    ext_ref[HALO_BLOCK:, :] = cur_ref[...]
    first = HALO_BLOCK - (CONV_KERNEL - 1) * SUBLANES
    rc = rows_per_chunk
    reps = rc // SUBLANES

    def chunk(ci, carry):
        r0 = pl.multiple_of(ci * rc, rc)
        cols = []
        for j in range(c // LANES):
            lanes = slice(j * LANES, (j + 1) * LANES)
            acc = jnp.broadcast_to(cb_ref[:, lanes], (rc, LANES))
            for k in range(CONV_KERNEL):
                w = jnp.tile(cw_ref[k, :, lanes], (reps, 1))
                acc = acc + w * ext_ref[pl.ds(r0 + first + SUBLANES * k, rc), lanes]
            cols.append(acc)
        acc = jnp.concatenate(cols, axis=-1)
        mu = jnp.mean(acc, axis=-1, keepdims=True)
        xc = acc - mu
        y = xc * lax.rsqrt(jnp.mean(xc * xc, axis=-1, keepdims=True) + LN_EPS)
        y = y * lg_ref[...] + lb_ref[...]
        act_ref[pl.ds(r0, rc), :] = jax.nn.silu(y).astype(BF16)
        return carry

    lax.fori_loop(0, tm // rc, chunk, 0)
    out_ref[...] = _dot(act_ref[...], wpw_ref[...]).astype(BF16)


def _conv_branch(u, conv_w, conv_b, ln_g, ln_b, w_pw, layer, tm):
    t, c = u.shape
    d = w_pw.shape[-1]
    ratio = tm // HALO_BLOCK
    return pl.pallas_call(
        functools.partial(_conv_kernel, rows_per_chunk=64),
        grid=(t // tm,),
        in_specs=[pl.BlockSpec((tm, c), lambda i: (i, 0)),
                  pl.BlockSpec((HALO_BLOCK, c), lambda i: (jnp.maximum(i * ratio - 1, 0), 0)),
                  pl.BlockSpec((None, CONV_KERNEL, SUBLANES, c), lambda i: (layer, 0, 0, 0)),
                  pl.BlockSpec((None, 1, c), lambda i: (layer, 0, 0)),
                  pl.BlockSpec((None, 1, c), lambda i: (layer, 0, 0)),
                  pl.BlockSpec((None, 1, c), lambda i: (layer, 0, 0)),
                  pl.BlockSpec((None, c, d), lambda i: (layer, 0, 0))],
        out_specs=pl.BlockSpec((tm, d), lambda i: (i, 0)),
        out_shape=jax.ShapeDtypeStruct((t, d), BF16),
        scratch_shapes=[pltpu.VMEM((tm + HALO_BLOCK, c), F32), pltpu.VMEM((tm, c), BF16)],
        compiler_params=_params("parallel"),
        name="conv_branch",
    )(u, u, conv_w, conv_b, ln_g, ln_b, w_pw)


def _ssm_kernel(u_ref, bm_ref, cm_ref, a_ref, d_ref, out_ref, hs_ref, st_ref, *, chains):
    step_idx = pl.program_id(0)
    rows, width = u_ref.shape
    hw = width // 2
    hs_cols = hs_ref.shape[1] // 2
    hstates = hs_cols // 2

    @pl.when(step_idx == 0)
    def _():
        st_ref[...] = jnp.zeros_like(st_ref)

    for hf in range(2):
        ub = u_ref[:, hf * hw:(hf + 1) * hw].astype(BF16)
        hs_ref[:, hf * hs_cols:(hf + 1) * hs_cols] = _dot(ub, bm_ref[hf])

    blocks = [(hf, j) for hf in range(2) for j in range(hstates // LANES)]
    for g0 in range(0, len(blocks), chains):
        cols = [(hf * hs_cols + j * LANES, hf * hs_cols + hstates + j * LANES, hf * hstates + j * LANES)
                for hf, j in blocks[g0:g0 + chains]]
        ars = [a_ref[0:SUBLANES, ac:ac + LANES] for _, _, ac in cols]
        ais = [a_ref[SUBLANES:2 * SUBLANES, ac:ac + LANES] for _, _, ac in cols]
        init = tuple((st_ref[:, cr:cr + LANES], st_ref[:, ci:ci + LANES]) for cr, ci, _ in cols)

        def step(t, carry, cols=cols, ars=ars, ais=ais):
            r0 = pl.multiple_of(t * SUBLANES, SUBLANES)
            new = []
            for n, (cr, ci, _) in enumerate(cols):
                hr, hi = carry[n]
                bur = hs_ref[pl.ds(r0, SUBLANES), cr:cr + LANES]
                bui = hs_ref[pl.ds(r0, SUBLANES), ci:ci + LANES]
                nr = ars[n] * hr - ais[n] * hi + bur
                ni = ars[n] * hi + ais[n] * hr + bui
                hs_ref[pl.ds(r0, SUBLANES), cr:cr + LANES] = nr
                hs_ref[pl.ds(r0, SUBLANES), ci:ci + LANES] = ni
                new.append((nr, ni))
            return tuple(new)

        fin = lax.fori_loop(0, rows // SUBLANES, step, init, unroll=8)
        for n, (cr, ci, _) in enumerate(cols):
            st_ref[:, cr:cr + LANES] = fin[n][0]
            st_ref[:, ci:ci + LANES] = fin[n][1]

    for hf in range(2):
        hb = hs_ref[:, hf * hs_cols:(hf + 1) * hs_cols].astype(BF16)
        y = _dot(hb, cm_ref[hf]) + d_ref[:, hf * hw:(hf + 1) * hw] * u_ref[:, hf * hw:(hf + 1) * hw]
        out_ref[:, hf * hw:(hf + 1) * hw] = jax.nn.gelu(y).astype(BF16)


def _ssm_branch(u, bm, cm, a_rows, d_row, rows):
    t, width = u.shape
    ncols = 2 * bm.shape[-1]
    return pl.pallas_call(
        functools.partial(_ssm_kernel, chains=4),
        grid=(t // rows,),
        in_specs=[pl.BlockSpec((rows, width), lambda i: (i, 0)),
                  pl.BlockSpec(bm.shape, lambda i: (0, 0, 0)),
                  pl.BlockSpec(cm.shape, lambda i: (0, 0, 0)),
                  pl.BlockSpec(a_rows.shape, lambda i: (0, 0)),
                  pl.BlockSpec(d_row.shape, lambda i: (0, 0))],
        out_specs=pl.BlockSpec((rows, width), lambda i: (i, 0)),
        out_shape=jax.ShapeDtypeStruct((t, width), BF16),
        scratch_shapes=[pltpu.VMEM((rows, ncols), F32), pltpu.VMEM((SUBLANES, ncols), F32)],
        compiler_params=_params("arbitrary"),
        name="ssm_branch",
    )(u, bm, cm, a_rows, d_row)


def _ssm_tables(lam_re, lam_im, log_dt, b_re, b_im, c_re, c_im, d_skip):
    g, p = lam_re.shape
    gh = g // 2
    dt = jnp.exp(log_dt)[:, None]
    mag = jnp.exp(lam_re * dt)
    ar = mag * jnp.cos(lam_im * dt)
    ai = mag * jnp.sin(lam_im * dt)
    den = lam_re * lam_re + lam_im * lam_im
    xr = ar - 1.0
    kr = (xr * lam_re + ai * lam_im) / den
    ki = (ai * lam_re - xr * lam_im) / den
    bbar_r = kr[..., None] * b_re - ki[..., None] * b_im
    bbar_i = kr[..., None] * b_im + ki[..., None] * b_re
    eye = jnp.eye(gh, dtype=F32)

    def bd_in(m):
        m = m.reshape(2, gh, p, SSM_GROUP)
        return jnp.einsum("aqph,qr->aqhrp", m, eye).reshape(2, gh * SSM_GROUP, gh * p)

    def bd_out(m):
        m = m.reshape(2, gh, SSM_GROUP, p)
        return jnp.einsum("aqhp,qr->aqprh", m, eye).reshape(2, gh * p, gh * SSM_GROUP)

    bm = jnp.concatenate([bd_in(bbar_r), bd_in(bbar_i)], axis=-1).astype(BF16)
    cm = jnp.concatenate([bd_out(c_re), -bd_out(c_im)], axis=1).astype(BF16)
    a_rows = jnp.concatenate([jnp.broadcast_to(ar.reshape(1, g * p), (SUBLANES, g * p)),
                              jnp.broadcast_to(ai.reshape(1, g * p), (SUBLANES, g * p))], axis=0)
    return bm, cm, a_rows, d_skip.reshape(1, g * SSM_GROUP)


def _batches_from_rows(val, slab_ref, out_ref):
    nb, n, d = out_ref.shape
    nslab = d // LANES
    for j in range(nslab):
        slab_ref[j] = val[:, j * LANES:(j + 1) * LANES]
    for b in range(nb):
        rows = [slab_ref[j, pl.ds(b, n, stride=nb), :] for j in range(nslab)]
        out_ref[b] = jnp.concatenate(rows, axis=-1).astype(out_ref.dtype)


def _rows_from_batches(in_ref, slab_ref):
    nb, n, d = in_ref.shape
    nslab = d // LANES
    for b in range(nb):
        xb = in_ref[b].astype(F32)
        for j in range(nslab):
            slab_ref[j, pl.ds(b, n, stride=nb), :] = xb[:, j * LANES:(j + 1) * LANES]
    return jnp.concatenate([slab_ref[j] for j in range(nslab)], axis=-1)


def _merge_kernel(ys_ref, wglu_ref, gt_ref, ya_ref, h_ref, wout_ref, gx_ref, wq_ref, hout_ref, q_ref,
                  slab_ref):
    d = h_ref.shape[1]
    z = _dot(ys_ref[...], wglu_ref[...])
    yb = z[:, :d] * jax.nn.sigmoid(z[:, d:])
    m = gt_ref[:, :d].astype(F32) * ya_ref[...].astype(F32) + gt_ref[:, d:].astype(F32) * yb
    h2 = h_ref[...] + _dot(m.astype(BF16), wout_ref[...])
    hout_ref[...] = h2
    _batches_from_rows(_dot(_rms(h2, gx_ref[...]).astype(BF16), wq_ref[...]), slab_ref, q_ref)


def _merge(ys, w_glu, gates, ya, h, w_out, gx, w_q, layer, nb, tm):
    t, d = h.shape
    sw = ys.shape[1]
    n = tm // nb
    return pl.pallas_call(
        _merge_kernel,
        grid=(t // tm,),
        in_specs=[pl.BlockSpec((tm, sw), lambda i: (i, 0)),
                  pl.BlockSpec((None, sw, 2 * d), lambda i: (layer, 0, 0)),
                  pl.BlockSpec((tm, 2 * d), lambda i: (i, 0)),
                  pl.BlockSpec((tm, d), lambda i: (i, 0)),
                  pl.BlockSpec((tm, d), lambda i: (i, 0)),
                  pl.BlockSpec((None, d, d), lambda i: (layer, 0, 0)),
                  pl.BlockSpec((None, 1, d), lambda i: (layer, 0, 0)),
                  pl.BlockSpec((None, d, d), lambda i: (layer, 0, 0))],
        out_specs=[pl.BlockSpec((tm, d), lambda i: (i, 0)),
                   pl.BlockSpec((nb, n, d), lambda i: (0, i, 0))],
        out_shape=[jax.ShapeDtypeStruct((t, d), F32), jax.ShapeDtypeStruct((nb, t // nb, d), BF16)],
        scratch_shapes=[pltpu.VMEM((d // LANES, tm, LANES), F32)],
        compiler_params=_params("parallel"),
        name="merge",
    )(ys, w_glu, gates, ya, h, w_out, gx, w_q)


def _kv_kernel(mem_ref, g_ref, w_ref, out_ref):
    xn = _rms(mem_ref[...], g_ref[...]).astype(BF16)
    out_ref[...] = _dot(xn, w_ref[...]).astype(BF16)


def _kv_proj(mem2d, g, w_kv):
    nl, d, d2 = w_kv.shape
    bm = mem2d.shape[0]
    return pl.pallas_call(
        _kv_kernel,
        grid=(nl, d2 // d),
        in_specs=[pl.BlockSpec((bm, d), lambda l, j: (0, 0)),
                  pl.BlockSpec((1, d), lambda l, j: (0, 0)),
                  pl.BlockSpec((None, d, d), lambda l, j: (l, 0, j))],
        out_specs=pl.BlockSpec((None, bm, d), lambda l, j: (l, 0, j)),
        out_shape=jax.ShapeDtypeStruct((nl, bm, d2), BF16),
        compiler_params=_params("parallel", "parallel"),
        name="kv_proj",
    )(mem2d, g, w_kv)


def _attn_kernel(q_ref, kv_ref, out_ref):
    d = q_ref.shape[1]
    hd = d // XATTN_HEADS
    scale = 1.0 / math.sqrt(hd)
    for n in range(XATTN_HEADS):
        qh = q_ref[:, n * hd:(n + 1) * hd]
        kh = kv_ref[:, n * hd:(n + 1) * hd]
        vh = kv_ref[:, d + n * hd:d + (n + 1) * hd]
        s = lax.dot_general(qh, kh, (((1,), (1,)), ((), ())), preferred_element_type=F32) * scale
        e = jnp.exp(s - jnp.max(s, axis=-1, keepdims=True))
        p = e / jnp.sum(e, axis=-1, keepdims=True)
        out_ref[:, n * hd:(n + 1) * hd] = _dot(p.astype(BF16), vh).astype(out_ref.dtype)


def _attention(q, kv, layer, ts):
    nb, s, d = q.shape
    m = kv.shape[1] // nb
    return pl.pallas_call(
        _attn_kernel,
        grid=(nb, s // ts),
        in_specs=[pl.BlockSpec((None, ts, d), lambda b, i: (b, i, 0)),
                  pl.BlockSpec((None, m, 2 * d), lambda b, i: (layer, b, 0))],
        out_specs=pl.BlockSpec((None, ts, d), lambda b, i: (b, i, 0)),
        out_shape=jax.ShapeDtypeStruct((nb, s, d), BF16),
        compiler_params=_params("parallel", "parallel"),
        name="xattn",
    )(q, kv)


def _attn_residual(o_ref, h_ref, wo_ref, slab_ref):
    o = _rows_from_batches(o_ref, slab_ref).astype(BF16)
    return h_ref[...] + _dot(o, wo_ref[...])


def _swiglu_step(xn, wg_ref, wu_ref, wd_ref):
    g = _dot(xn, wg_ref[...].astype(BF16))
    u = _dot(xn, wu_ref[...].astype(BF16))
    return _dot((jax.nn.silu(g) * u).astype(BF16), wd_ref[...].astype(BF16))


def _ffn_kernel(o_ref, h_ref, wo_ref, g_ref, wg_ref, wu_ref, wd_ref, out_ref, slab_ref, xn_ref, acc_ref):
    f = pl.program_id(1)

    @pl.when(f == 0)
    def _():
        h2 = _attn_residual(o_ref, h_ref, wo_ref, slab_ref)
        xn_ref[...] = _rms(h2, g_ref[...]).astype(BF16)
        acc_ref[...] = h2

    acc_ref[...] += _swiglu_step(xn_ref[...], wg_ref, wu_ref, wd_ref)

    @pl.when(f == pl.num_programs(1) - 1)
    def _():
        out_ref[...] = acc_ref[...]


def _ff_tile(ff):
    for tf in (512, 256, 128):
        if ff % tf == 0:
            return tf
    raise ValueError(f"hidden size {ff} is not a multiple of {LANES}")


def _dense_ffn(o, h, w_o, g, w_gu, w_d, layer, j, tm):
    t, d = h.shape
    nb = o.shape[0]
    n = tm // nb
    ff = w_d.shape[1]
    tf = _ff_tile(ff)
    nf = ff // tf
    return pl.pallas_call(
        _ffn_kernel,
        grid=(t // tm, nf),
        in_specs=[pl.BlockSpec((nb, n, d), lambda i, f: (0, i, 0)),
                  pl.BlockSpec((tm, d), lambda i, f: (i, 0)),
                  pl.BlockSpec((None, d, d), lambda i, f: (layer, 0, 0), pipeline_mode=pl.Buffered(1)),
                  pl.BlockSpec((None, 1, d), lambda i, f: (layer, 0, 0)),
                  pl.BlockSpec((None, d, tf), lambda i, f: (j, 0, f)),
                  pl.BlockSpec((None, d, tf), lambda i, f: (j, 0, nf + f)),
                  pl.BlockSpec((None, tf, d), lambda i, f: (j, f, 0))],
        out_specs=pl.BlockSpec((tm, d), lambda i, f: (i, 0)),
        out_shape=jax.ShapeDtypeStruct((t, d), F32),
        scratch_shapes=[pltpu.VMEM((d // LANES, tm, LANES), F32), pltpu.VMEM((tm, d), BF16),
                        pltpu.VMEM((tm, d), F32)],
        compiler_params=_params("parallel", "arbitrary"),
        name="dense_ffn",
    )(o, h, w_o, g, w_gu, w_gu, w_d)


def _router_kernel(o_ref, h_ref, wo_ref, g_ref, rw_ref, rb_ref, h2_ref, hn_ref, idx_ref, wt_ref, slab_ref):
    h2 = _attn_residual(o_ref, h_ref, wo_ref, slab_ref)
    h2_ref[...] = h2
    xn = _rms(h2, g_ref[...])
    xh = xn.astype(BF16)
    hn_ref[...] = xh
    xl = (xn - xh.astype(F32)).astype(BF16)
    ph = _dot(xh, rw_ref[...])
    logits = ph[:, :LANES] + ph[:, LANES:] + _dot(xl, rw_ref[:, :LANES]) + rb_ref[...]
    lane = lax.broadcasted_iota(jnp.int32, logits.shape, 1)
    lanef = lane.astype(F32)
    neg = jnp.float32(-jnp.inf)
    big = jnp.float32(LANES)
    l1 = jnp.where(lane < N_EXPERTS, logits, neg)
    m1 = jnp.max(l1, axis=-1, keepdims=True)
    i1 = jnp.min(jnp.where(l1 == m1, lanef, big), axis=-1, keepdims=True)
    l2 = jnp.where(lanef == i1, neg, l1)
    m2 = jnp.max(l2, axis=-1, keepdims=True)
    i2 = jnp.min(jnp.where(l2 == m2, lanef, big), axis=-1, keepdims=True)
    e = jnp.exp(m2 - m1)
    w1 = 1.0 / (1.0 + e)
    w2 = e / (1.0 + e)
    idx_ref[...] = jnp.where(lane == 0, i1, jnp.where(lane == 1, i2, 0.0)).astype(jnp.int32)
    wt_ref[...] = jnp.where(lane == 0, w1, jnp.where(lane == 1, w2, 0.0))


def _router(o, h, w_o, g, rw, rb, layer, tm):
    t, d = h.shape
    nb = o.shape[0]
    n = tm // nb
    return pl.pallas_call(
        _router_kernel,
        grid=(t // tm,),
        in_specs=[pl.BlockSpec((nb, n, d), lambda i: (0, i, 0)),
                  pl.BlockSpec((tm, d), lambda i: (i, 0)),
                  pl.BlockSpec((None, d, d), lambda i: (layer, 0, 0)),
                  pl.BlockSpec((None, 1, d), lambda i: (layer, 0, 0)),
                  pl.BlockSpec((d, 2 * LANES), lambda i: (0, 0)),
                  pl.BlockSpec((1, LANES), lambda i: (0, 0))],
        out_specs=[pl.BlockSpec((tm, d), lambda i: (i, 0)),
                   pl.BlockSpec((tm, d), lambda i: (i, 0)),
                   pl.BlockSpec((tm, LANES), lambda i: (i, 0)),
                   pl.BlockSpec((tm, LANES), lambda i: (i, 0))],
        out_shape=[jax.ShapeDtypeStruct((t, d), F32),
                   jax.ShapeDtypeStruct((t, d), BF16),
                   jax.ShapeDtypeStruct((t, LANES), jnp.int32),
                   jax.ShapeDtypeStruct((t, LANES), F32)],
        scratch_shapes=[pltpu.VMEM((d // LANES, tm, LANES), F32)],
        compiler_params=_params("parallel"),
        name="router",
    )(o, h, w_o, g, rw, rb)


def _moe_kernel(te_ref, tv_ref, x_ref, wg_ref, wu_ref, wd_ref, out_ref, acc_ref):
    i = pl.program_id(0)
    f = pl.program_id(1)
    last = pl.num_programs(1) - 1
    valid = tv_ref[i] > 0

    @pl.when(f == 0)
    def _():
        acc_ref[...] = jnp.zeros_like(acc_ref)

    @pl.when(valid)
    def _():
        acc_ref[...] += _swiglu_step(x_ref[...], wg_ref, wu_ref, wd_ref)

    @pl.when(f == last)
    def _():
        out_ref[...] = acc_ref[...]


def _moe_experts(xg, w_gu, w_d, tile_expert, tile_valid, j, tm):
    r, d = xg.shape
    ff = w_d.shape[2]
    tf = _ff_tile(ff)
    nf = ff // tf
    grid_spec = pltpu.PrefetchScalarGridSpec(
        num_scalar_prefetch=2,
        grid=(r // tm, nf),
        in_specs=[pl.BlockSpec((tm, d), lambda i, f, te, tv: (i, 0)),
                  pl.BlockSpec((None, None, d, tf), lambda i, f, te, tv: (j, te[i], 0, f)),
                  pl.BlockSpec((None, None, d, tf), lambda i, f, te, tv: (j, te[i], 0, nf + f)),
                  pl.BlockSpec((None, None, tf, d), lambda i, f, te, tv: (j, te[i], f, 0))],
        out_specs=pl.BlockSpec((tm, d), lambda i, f, te, tv: (i, 0)),
        scratch_shapes=[pltpu.VMEM((tm, d), F32)])
    return pl.pallas_call(
        _moe_kernel,
        grid_spec=grid_spec,
        out_shape=jax.ShapeDtypeStruct((r, d), F32),
        compiler_params=_params("parallel", "arbitrary"),
        name="moe_experts",
    )(tile_expert, tile_valid, xg, w_gu, w_gu, w_d)


def _dispatch_tables(idx, tm):
    t = idx.shape[0]
    npairs = t * TOP_K
    ntiles = npairs // tm + N_EXPERTS
    e_flat = idx.reshape(npairs)
    onehot = (e_flat[:, None] == jnp.arange(N_EXPERTS, dtype=jnp.int32)[None, :]).astype(jnp.int32)
    csum = jnp.cumsum(onehot, axis=0)
    counts = csum[-1]
    rank = jnp.take_along_axis(csum, e_flat[:, None], axis=1)[:, 0] - 1
    padded = ((counts + tm - 1) // tm) * tm
    pend = jnp.cumsum(padded)
    pstart = pend - padded
    cstart = jnp.cumsum(counts) - counts
    pos = pstart[e_flat] + rank
    order = jnp.argsort(e_flat * npairs + jnp.arange(npairs, dtype=jnp.int32))
    tile_start = jnp.arange(ntiles, dtype=jnp.int32) * tm
    tile_expert = jnp.minimum(jnp.searchsorted(pend, tile_start, side="right"),
                              N_EXPERTS - 1).astype(jnp.int32)
    tile_valid = (tile_start < pend[-1]).astype(jnp.int32)
    slot = jnp.arange(ntiles * tm, dtype=jnp.int32)
    slot_e = jnp.repeat(tile_expert, tm)
    within = slot - pstart[slot_e]
    src = jnp.clip(cstart[slot_e] + within, 0, npairs - 1)
    row_token = jnp.where(within < counts[slot_e], order[src] // TOP_K, 0).astype(jnp.int32)
    return row_token, pos.reshape(t, TOP_K), tile_expert, tile_valid


def _final_kernel(h_ref, g_ref, out_ref):
    out_ref[...] = _rms(h_ref[...], g_ref[...])


def _final_norm(h, g, tm):
    t, d = h.shape
    return pl.pallas_call(
        _final_kernel,
        grid=(t // tm,),
        in_specs=[pl.BlockSpec((tm, d), lambda i: (i, 0)), pl.BlockSpec((1, d), lambda i: (0, 0))],
        out_specs=pl.BlockSpec((tm, d), lambda i: (i, 0)),
        out_shape=jax.ShapeDtypeStruct((t, d), F32),
        compiler_params=_params("parallel"),
        name="final_norm",
    )(h, g)


def kernel(x, mem, mem_norm_g, norm_mix_g, w_in, conv_w, conv_b, conv_ln_g, conv_ln_b, w_conv_pw,
           ssm_lambda_re, ssm_lambda_im, ssm_log_dt, ssm_b_re, ssm_b_im, ssm_c_re, ssm_c_im, ssm_d,
           w_ssm_glu, w_out, norm_xattn_g, w_q, w_kv, w_o, norm_ffn_g, ffn_w_gate_up, ffn_w_down,
           router_w, router_b, moe_w_gate_up, moe_w_down, final_norm_g):
    nb, s, d = x.shape
    assert nb == SUBLANES, "the row layout puts the batch on the sublane axis"
    depth = w_in.shape[0]
    cw = conv_w.shape[-1]
    sw = ssm_d.shape[1] * ssm_d.shape[2]
    t = nb * s
    tm = min(512, t)
    tm_ff = min(1024, t)
    ts = min(512, s)

    w_in_b = w_in.astype(BF16)
    w_pw_b = w_conv_pw.astype(BF16)
    w_glu_b = w_ssm_glu.astype(BF16)
    w_out_b = w_out.astype(BF16)
    w_q_b = w_q.astype(BF16)
    w_kv_b = w_kv.astype(BF16)
    w_o_b = w_o.astype(BF16)
    ffn_gu_b = ffn_w_gate_up.astype(BF16)
    ffn_d_b = ffn_w_down.astype(BF16)
    conv_w_rep = jnp.broadcast_to(conv_w[:, :, None, :], conv_w.shape[:2] + (SUBLANES, cw))
    row3 = lambda a: a.reshape(a.shape[0], 1, a.shape[1])
    g_mix, g_x, g_ffn = row3(norm_mix_g), row3(norm_xattn_g), row3(norm_ffn_g)
    cb3, lg3, lb3 = row3(conv_b), row3(conv_ln_g), row3(conv_ln_b)
    rw_pad = jnp.pad(router_w, ((0, 0), (0, 0), (0, LANES - N_EXPERTS)))
    rw_hi = rw_pad.astype(BF16)
    rw_pad = jnp.concatenate([rw_hi, (rw_pad - rw_hi.astype(F32)).astype(BF16)], axis=-1)
    rb_pad = jnp.pad(router_b, ((0, 0), (0, LANES - N_EXPERTS)))

    kv = _kv_proj(mem.reshape(nb * mem.shape[1], d), mem_norm_g.reshape(1, d), w_kv_b)
    h = x.transpose(1, 0, 2).reshape(t, d)

    for i in range(depth):
        u_conv, u_ssm, gates = _in_proj(h, g_mix, w_in_b, i, cw, sw, tm)
        y_a = _conv_branch(u_conv, conv_w_rep, cb3, lg3, lb3, w_pw_b, i, tm)
        tables = _ssm_tables(ssm_lambda_re[i], ssm_lambda_im[i], ssm_log_dt[i], ssm_b_re[i], ssm_b_im[i],
                             ssm_c_re[i], ssm_c_im[i], ssm_d[i])
        y_s = _ssm_branch(u_ssm, *tables, rows=tm)
        h, q = _merge(y_s, w_glu_b, gates, y_a, h, w_out_b, g_x, w_q_b, i, nb, tm)
        o = _attention(q, kv, i, ts)
        j = i // 2
        if i % 2 == 0:
            h = _dense_ffn(o, h, w_o_b, g_ffn, ffn_gu_b, ffn_d_b, i, j, tm_ff)
        else:
            h, hn, idx, wts = _router(o, h, w_o_b, g_ffn, rw_pad[j], rb_pad[j:j + 1], i, tm)
            row_token, pos, tile_expert, tile_valid = _dispatch_tables(idx[:, :TOP_K], tm_ff)
            xg = jnp.take(hn, row_token, axis=0)
            yg = _moe_experts(xg, moe_w_gate_up, moe_w_down, tile_expert, tile_valid, j, tm_ff)
            h = (h + wts[:, 0:1] * jnp.take(yg, pos[:, 0], axis=0)
                 + wts[:, 1:2] * jnp.take(yg, pos[:, 1], axis=0))

    out = _final_norm(h, final_norm_g.reshape(1, d), tm)
    return out.reshape(s, nb, d).transpose(1, 0, 2)
```

```python
import functools
import math

import jax
import jax.numpy as jnp
from jax import lax
from jax.experimental import pallas as pl
from jax.experimental.pallas import tpu as pltpu

F32 = jnp.float32
BF16 = jnp.bfloat16

RMS_EPS = 1e-6
LN_EPS = 1e-5
CONV_KERNEL = 31
SSM_GROUP = 16
SSM_STATE = 64
XATTN_HEADS = 4
N_EXPERTS = 8
TOP_K = 2

SUBLANES = 8
LANES = 128
HALO_BLOCK = 256
VMEM_LIMIT = 52 * 2 ** 20


def _params(*sem):
    return pltpu.CompilerParams(dimension_semantics=sem, vmem_limit_bytes=VMEM_LIMIT)


def _rms(x, g):
    return x * lax.rsqrt(jnp.mean(x * x, axis=-1, keepdims=True) + RMS_EPS) * g


def _dot(a, b):
    return jnp.dot(a, b, preferred_element_type=F32)


def _in_kernel(h_ref, g_ref, w_ref, uc_ref, us_ref, gt_ref, *rest, cw, sw):
    if rest:
        hrow_ref, slab_ref = rest
        x = _rows_from_batches(h_ref, slab_ref)
        hrow_ref[...] = x
    else:
        x = h_ref[...]
    xn = _rms(x, g_ref[...]).astype(BF16)
    a = _dot(xn, w_ref[:, 0:cw])
    gate = _dot(xn, w_ref[:, cw:2 * cw])
    uc_ref[...] = a * jax.nn.sigmoid(gate)
    us_ref[...] = _dot(xn, w_ref[:, 2 * cw:2 * cw + sw])
    gt_ref[...] = jax.nn.sigmoid(_dot(xn, w_ref[:, 2 * cw + sw:])).astype(BF16)


def _in_proj(h, g, w, layer, cw, sw, tm):
    per_batch = h.ndim == 3
    d = h.shape[-1]
    t = h.shape[0] * h.shape[1] if per_batch else h.shape[0]
    cols = w.shape[-1]
    ng = cols - 2 * cw - sw
    if per_batch:
        nb = h.shape[0]
        h_spec = pl.BlockSpec((nb, tm // nb, d), lambda i: (0, i, 0))
    else:
        h_spec = pl.BlockSpec((tm, d), lambda i: (i, 0))
    out_specs = [pl.BlockSpec((tm, cw), lambda i: (i, 0)),
                 pl.BlockSpec((tm, sw), lambda i: (i, 0)),
                 pl.BlockSpec((tm, ng), lambda i: (i, 0))]
    out_shape = [jax.ShapeDtypeStruct((t, cw), F32),
                 jax.ShapeDtypeStruct((t, sw), F32),
                 jax.ShapeDtypeStruct((t, ng), BF16)]
    scratch = []
    if per_batch:
        out_specs.append(pl.BlockSpec((tm, d), lambda i: (i, 0)))
        out_shape.append(jax.ShapeDtypeStruct((t, d), F32))
        scratch.append(pltpu.VMEM((d // LANES, tm, LANES), F32))
    return pl.pallas_call(
        functools.partial(_in_kernel, cw=cw, sw=sw),
        grid=(t // tm,),
        in_specs=[h_spec,
                  pl.BlockSpec((None, 1, d), lambda i: (layer, 0, 0)),
                  pl.BlockSpec((None, d, cols), lambda i: (layer, 0, 0))],
        out_specs=out_specs,
        out_shape=out_shape,
        scratch_shapes=scratch,
        compiler_params=_params("parallel"),
        name="in_proj",
    )(h, g, w)


def _conv_kernel(cur_ref, prev_ref, cw_ref, cb_ref, lg_ref, lb_ref, wpw_ref, out_ref,
                 ext_ref, conv_ref, act_ref, *, rows_per_chunk):
    i = pl.program_id(0)
    tm, c = cur_ref.shape
    nslab = c // LANES
    prev = jnp.where(i > 0, prev_ref[...], 0.0)
    for j in range(nslab):
        ext_ref[j, 0:HALO_BLOCK, :] = prev[:, j * LANES:(j + 1) * LANES]
        ext_ref[j, HALO_BLOCK:, :] = cur_ref[:, j * LANES:(j + 1) * LANES]
    first = HALO_BLOCK - (CONV_KERNEL - 1) * SUBLANES
    rc = rows_per_chunk
    rcb = rc
    reps = rcb // SUBLANES

    def conv_block(n, carry):
        ci = n // nslab
        j = n % nslab
        r0 = pl.multiple_of(ci * rcb, rcb)
        accs = [cb_ref[j]] * reps
        for k in range(CONV_KERNEL):
            w = cw_ref[k * nslab + j]
            accs = [a + w * ext_ref[j, pl.ds(r0 + first + SUBLANES * (k + r), SUBLANES), :]
                    for r, a in enumerate(accs)]
        conv_ref[j, pl.ds(r0, rcb), :] = jnp.concatenate(accs, axis=0)
        return carry

    lax.fori_loop(0, (tm // rcb) * nslab, conv_block, 0)

    def norm_chunk(ci, carry):
        r0 = pl.multiple_of(ci * rc, rc)
        acc = jnp.concatenate([conv_ref[j, pl.ds(r0, rc), :] for j in range(nslab)], axis=-1)
        mu = jnp.mean(acc, axis=-1, keepdims=True)
        xc = acc - mu
        y = xc * lax.rsqrt(jnp.mean(xc * xc, axis=-1, keepdims=True) + LN_EPS)
        y = y * lg_ref[...] + lb_ref[...]
        act_ref[pl.ds(r0, rc), :] = jax.nn.silu(y).astype(BF16)
        return carry

    lax.fori_loop(0, tm // rc, norm_chunk, 0, unroll=4)
    out_ref[...] = _dot(act_ref[...], wpw_ref[...]).astype(BF16)


def _conv_branch(u, conv_w, conv_b, ln_g, ln_b, w_pw, layer, tm):
    t, c = u.shape
    d = w_pw.shape[-1]
    ratio = tm // HALO_BLOCK
    return pl.pallas_call(
        functools.partial(_conv_kernel, rows_per_chunk=64),
        grid=(t // tm,),
        in_specs=[pl.BlockSpec((tm, c), lambda i: (i, 0)),
                  pl.BlockSpec((HALO_BLOCK, c), lambda i: (jnp.maximum(i * ratio - 1, 0), 0)),
                  pl.BlockSpec((None,) + conv_w.shape[1:], lambda i: (layer, 0, 0, 0)),
                  pl.BlockSpec((None,) + conv_b.shape[1:], lambda i: (layer, 0, 0, 0)),
                  pl.BlockSpec((None, 1, c), lambda i: (layer, 0, 0)),
                  pl.BlockSpec((None, 1, c), lambda i: (layer, 0, 0)),
                  pl.BlockSpec((None, c, d), lambda i: (layer, 0, 0))],
        out_specs=pl.BlockSpec((tm, d), lambda i: (i, 0)),
        out_shape=jax.ShapeDtypeStruct((t, d), BF16),
        scratch_shapes=[pltpu.VMEM((c // LANES, tm + HALO_BLOCK, LANES), F32),
                        pltpu.VMEM((c // LANES, tm, LANES), F32), pltpu.VMEM((tm, c), BF16)],
        compiler_params=_params("parallel"),
        name="conv_branch",
    )(u, u, conv_w, conv_b, ln_g, ln_b, w_pw)


def _ssm_kernel(u_ref, bm_ref, cm_ref, a_ref, d_ref, out_ref, hs_ref, st_ref, *, chains):
    step_idx = pl.program_id(0)
    rows, width = u_ref.shape
    hw = width // 2
    hs_cols = hs_ref.shape[1] // 2
    hstates = hs_cols // 2

    @pl.when(step_idx == 0)
    def _():
        st_ref[...] = jnp.zeros_like(st_ref)

    for hf in range(2):
        ub = u_ref[:, hf * hw:(hf + 1) * hw].astype(BF16)
        hs_ref[:, hf * hs_cols:(hf + 1) * hs_cols] = _dot(ub, bm_ref[hf])

    blocks = [(hf, j) for hf in range(2) for j in range(hstates // LANES)]
    for g0 in range(0, len(blocks), chains):
        cols = [(hf * hs_cols + j * LANES, hf * hs_cols + hstates + j * LANES, hf * hstates + j * LANES)
                for hf, j in blocks[g0:g0 + chains]]
        ars = [a_ref[0:SUBLANES, ac:ac + LANES] for _, _, ac in cols]
        ais = [a_ref[SUBLANES:2 * SUBLANES, ac:ac + LANES] for _, _, ac in cols]
        init = tuple((st_ref[:, cr:cr + LANES], st_ref[:, ci:ci + LANES]) for cr, ci, _ in cols)

        def step(t, carry, cols=cols, ars=ars, ais=ais):
            r0 = pl.multiple_of(t * SUBLANES, SUBLANES)
            new = []
            for n, (cr, ci, _) in enumerate(cols):
                hr, hi = carry[n]
                bur = hs_ref[pl.ds(r0, SUBLANES), cr:cr + LANES]
                bui = hs_ref[pl.ds(r0, SUBLANES), ci:ci + LANES]
                nr = ars[n] * hr - ais[n] * hi + bur
                ni = ars[n] * hi + ais[n] * hr + bui
                hs_ref[pl.ds(r0, SUBLANES), cr:cr + LANES] = nr
                hs_ref[pl.ds(r0, SUBLANES), ci:ci + LANES] = ni
                new.append((nr, ni))
            return tuple(new)

        fin = lax.fori_loop(0, rows // SUBLANES, step, init, unroll=8)
        for n, (cr, ci, _) in enumerate(cols):
            st_ref[:, cr:cr + LANES] = fin[n][0]
            st_ref[:, ci:ci + LANES] = fin[n][1]

    for hf in range(2):
        hb = hs_ref[:, hf * hs_cols:(hf + 1) * hs_cols].astype(BF16)
        y = _dot(hb, cm_ref[hf]) + d_ref[:, hf * hw:(hf + 1) * hw] * u_ref[:, hf * hw:(hf + 1) * hw]
        out_ref[:, hf * hw:(hf + 1) * hw] = jax.nn.gelu(y).astype(BF16)


def _ssm_branch(u, bm, cm, a_rows, d_row, rows):
    t, width = u.shape
    ncols = 2 * bm.shape[-1]
    return pl.pallas_call(
        functools.partial(_ssm_kernel, chains=4),
        grid=(t // rows,),
        in_specs=[pl.BlockSpec((rows, width), lambda i: (i, 0)),
                  pl.BlockSpec(bm.shape, lambda i: (0, 0, 0)),
                  pl.BlockSpec(cm.shape, lambda i: (0, 0, 0)),
                  pl.BlockSpec(a_rows.shape, lambda i: (0, 0)),
                  pl.BlockSpec(d_row.shape, lambda i: (0, 0))],
        out_specs=pl.BlockSpec((rows, width), lambda i: (i, 0)),
        out_shape=jax.ShapeDtypeStruct((t, width), BF16),
        scratch_shapes=[pltpu.VMEM((rows, ncols), F32), pltpu.VMEM((SUBLANES, ncols), F32)],
        compiler_params=_params("arbitrary"),
        name="ssm_branch",
    )(u, bm, cm, a_rows, d_row)


def _ssm_tables(lam_re, lam_im, log_dt, b_re, b_im, c_re, c_im, d_skip):
    g, p = lam_re.shape
    gh = g // 2
    dt = jnp.exp(log_dt)[:, None]
    mag = jnp.exp(lam_re * dt)
    ar = mag * jnp.cos(lam_im * dt)
    ai = mag * jnp.sin(lam_im * dt)
    den = lam_re * lam_re + lam_im * lam_im
    xr = ar - 1.0
    kr = (xr * lam_re + ai * lam_im) / den
    ki = (ai * lam_re - xr * lam_im) / den
    bbar_r = kr[..., None] * b_re - ki[..., None] * b_im
    bbar_i = kr[..., None] * b_im + ki[..., None] * b_re
    eye = jnp.eye(gh, dtype=F32)

    def bd_in(m):
        m = m.reshape(2, gh, p, SSM_GROUP)
        return jnp.einsum("aqph,qr->aqhrp", m, eye).reshape(2, gh * SSM_GROUP, gh * p)

    def bd_out(m):
        m = m.reshape(2, gh, SSM_GROUP, p)
        return jnp.einsum("aqhp,qr->aqprh", m, eye).reshape(2, gh * p, gh * SSM_GROUP)

    bm = jnp.concatenate([bd_in(bbar_r), bd_in(bbar_i)], axis=-1).astype(BF16)
    cm = jnp.concatenate([bd_out(c_re), -bd_out(c_im)], axis=1).astype(BF16)
    a_rows = jnp.concatenate([jnp.broadcast_to(ar.reshape(1, g * p), (SUBLANES, g * p)),
                              jnp.broadcast_to(ai.reshape(1, g * p), (SUBLANES, g * p))], axis=0)
    return bm, cm, a_rows, d_skip.reshape(1, g * SSM_GROUP)


def _batches_from_rows(val, slab_ref, out_ref):
    nb, n, d = out_ref.shape
    nslab = d // LANES
    for j in range(nslab):
        slab_ref[j] = val[:, j * LANES:(j + 1) * LANES]
    for b in range(nb):
        rows = [slab_ref[j, pl.ds(b, n, stride=nb), :] for j in range(nslab)]
        out_ref[b] = jnp.concatenate(rows, axis=-1).astype(out_ref.dtype)


def _rows_from_batches(in_ref, slab_ref):
    nb, n, d = in_ref.shape
    nslab = d // LANES
    for b in range(nb):
        xb = in_ref[b].astype(F32)
        for j in range(nslab):
            slab_ref[j, pl.ds(b, n, stride=nb), :] = xb[:, j * LANES:(j + 1) * LANES]
    return jnp.concatenate([slab_ref[j] for j in range(nslab)], axis=-1)


def _merge_kernel(ys_ref, wglu_ref, gt_ref, ya_ref, h_ref, wout_ref, gx_ref, wq_ref, hout_ref, q_ref,
                  slab_ref):
    d = h_ref.shape[1]
    z = _dot(ys_ref[...], wglu_ref[...])
    yb = z[:, :d] * jax.nn.sigmoid(z[:, d:])
    m = gt_ref[:, :d].astype(F32) * ya_ref[...].astype(F32) + gt_ref[:, d:].astype(F32) * yb
    h2 = h_ref[...] + _dot(m.astype(BF16), wout_ref[...])
    hout_ref[...] = h2
    _batches_from_rows(_dot(_rms(h2, gx_ref[...]).astype(BF16), wq_ref[...]), slab_ref, q_ref)


def _merge(ys, w_glu, gates, ya, h, w_out, gx, w_q, layer, nb, tm):
    t, d = h.shape
    sw = ys.shape[1]
    n = tm // nb
    return pl.pallas_call(
        _merge_kernel,
        grid=(t // tm,),
        in_specs=[pl.BlockSpec((tm, sw), lambda i: (i, 0)),
                  pl.BlockSpec((None, sw, 2 * d), lambda i: (layer, 0, 0)),
                  pl.BlockSpec((tm, 2 * d), lambda i: (i, 0)),
                  pl.BlockSpec((tm, d), lambda i: (i, 0)),
                  pl.BlockSpec((tm, d), lambda i: (i, 0)),
                  pl.BlockSpec((None, d, d), lambda i: (layer, 0, 0)),
                  pl.BlockSpec((None, 1, d), lambda i: (layer, 0, 0)),
                  pl.BlockSpec((None, d, d), lambda i: (layer, 0, 0))],
        out_specs=[pl.BlockSpec((tm, d), lambda i: (i, 0)),
                   pl.BlockSpec((nb, n, d), lambda i: (0, i, 0))],
        out_shape=[jax.ShapeDtypeStruct((t, d), F32), jax.ShapeDtypeStruct((nb, t // nb, d), BF16)],
        scratch_shapes=[pltpu.VMEM((d // LANES, tm, LANES), F32)],
        compiler_params=_params("parallel"),
        name="merge",
    )(ys, w_glu, gates, ya, h, w_out, gx, w_q)


def _kv_kernel(mem_ref, g_ref, w_ref, out_ref):
    xn = _rms(mem_ref[...], g_ref[...]).astype(BF16)
    out_ref[...] = _dot(xn, w_ref[...]).astype(BF16)


def _kv_proj(mem2d, g, w_kv):
    nl, d, d2 = w_kv.shape
    bm = mem2d.shape[0]
    return pl.pallas_call(
        _kv_kernel,
        grid=(nl, d2 // d),
        in_specs=[pl.BlockSpec((bm, d), lambda l, j: (0, 0)),
                  pl.BlockSpec((1, d), lambda l, j: (0, 0)),
                  pl.BlockSpec((None, d, d), lambda l, j: (l, 0, j))],
        out_specs=pl.BlockSpec((None, bm, d), lambda l, j: (l, 0, j)),
        out_shape=jax.ShapeDtypeStruct((nl, bm, d2), BF16),
        compiler_params=_params("parallel", "parallel"),
        name="kv_proj",
    )(mem2d, g, w_kv)


def _attn_kernel(q_ref, kv_ref, out_ref):
    d = q_ref.shape[1]
    hd = d // XATTN_HEADS
    scale = 1.0 / math.sqrt(hd)
    for n in range(XATTN_HEADS):
        qh = q_ref[:, n * hd:(n + 1) * hd]
        kh = kv_ref[:, n * hd:(n + 1) * hd]
        vh = kv_ref[:, d + n * hd:d + (n + 1) * hd]
        s = lax.dot_general(qh, kh, (((1,), (1,)), ((), ())), preferred_element_type=F32) * scale
        e = jnp.exp(s - jnp.max(s, axis=-1, keepdims=True))
        p = e / jnp.sum(e, axis=-1, keepdims=True)
        out_ref[:, n * hd:(n + 1) * hd] = _dot(p.astype(BF16), vh).astype(out_ref.dtype)


def _attention(q, kv, layer, ts):
    nb, s, d = q.shape
    m = kv.shape[1] // nb
    return pl.pallas_call(
        _attn_kernel,
        grid=(nb, s // ts),
        in_specs=[pl.BlockSpec((None, ts, d), lambda b, i: (b, i, 0)),
                  pl.BlockSpec((None, m, 2 * d), lambda b, i: (layer, b, 0))],
        out_specs=pl.BlockSpec((None, ts, d), lambda b, i: (b, i, 0)),
        out_shape=jax.ShapeDtypeStruct((nb, s, d), BF16),
        compiler_params=_params("parallel", "parallel"),
        name="xattn",
    )(q, kv)


def _attn_residual(o_ref, h_ref, wo_ref, slab_ref):
    o = _rows_from_batches(o_ref, slab_ref).astype(BF16)
    return h_ref[...] + _dot(o, wo_ref[...])


def _swiglu_step(xn, wg_ref, wu_ref, wd_ref):
    g = _dot(xn, wg_ref[...].astype(BF16))
    u = _dot(xn, wu_ref[...].astype(BF16))
    return _dot((jax.nn.silu(g) * u).astype(BF16), wd_ref[...].astype(BF16))


def _ffn_kernel(o_ref, h_ref, wo_ref, g_ref, wg_ref, wu_ref, wd_ref, out_ref, slab_ref, xn_ref):
    @pl.when(pl.program_id(1) == 0)
    def _():
        h2 = _attn_residual(o_ref, h_ref, wo_ref, slab_ref)
        xn_ref[...] = _rms(h2, g_ref[...]).astype(BF16)
        out_ref[...] = h2

    out_ref[...] += _swiglu_step(xn_ref[...], wg_ref, wu_ref, wd_ref)


def _ff_tile(ff):
    for tf in (512, 256, 128):
        if ff % tf == 0:
            return tf
    raise ValueError(f"hidden size {ff} is not a multiple of {LANES}")


def _dense_ffn(o, h, w_o, g, w_gu, w_d, layer, j, tm):
    t, d = h.shape
    nb = o.shape[0]
    n = tm // nb
    ff = w_d.shape[1]
    tf = _ff_tile(ff)
    nf = ff // tf
    return pl.pallas_call(
        _ffn_kernel,
        grid=(t // tm, nf),
        in_specs=[pl.BlockSpec((nb, n, d), lambda i, f: (0, i, 0)),
                  pl.BlockSpec((tm, d), lambda i, f: (i, 0)),
                  pl.BlockSpec((None, d, d), lambda i, f: (layer, 0, 0), pipeline_mode=pl.Buffered(1)),
                  pl.BlockSpec((None, 1, d), lambda i, f: (layer, 0, 0)),
                  pl.BlockSpec((None, d, tf), lambda i, f: (j, 0, f)),
                  pl.BlockSpec((None, d, tf), lambda i, f: (j, 0, nf + f)),
                  pl.BlockSpec((None, tf, d), lambda i, f: (j, f, 0))],
        out_specs=pl.BlockSpec((tm, d), lambda i, f: (i, 0)),
        out_shape=jax.ShapeDtypeStruct((t, d), F32),
        scratch_shapes=[pltpu.VMEM((d // LANES, tm, LANES), F32), pltpu.VMEM((tm, d), BF16)],
        compiler_params=_params("parallel", "arbitrary"),
        name="dense_ffn",
    )(o, h, w_o, g, w_gu, w_gu, w_d)


def _router_kernel(o_ref, h_ref, wo_ref, g_ref, rw_ref, rb_ref, h2_ref, hn_ref, idx_ref, wt_ref, slab_ref):
    h2 = _attn_residual(o_ref, h_ref, wo_ref, slab_ref)
    h2_ref[...] = h2
    xn = _rms(h2, g_ref[...])
    xh = xn.astype(BF16)
    hn_ref[...] = xn
    xl = (xn - xh.astype(F32)).astype(BF16)
    ph = _dot(xh, rw_ref[...])
    logits = ph[:, :LANES] + ph[:, LANES:] + _dot(xl, rw_ref[:, :LANES]) + rb_ref[...]
    lane = lax.broadcasted_iota(jnp.int32, logits.shape, 1)
    lanef = lane.astype(F32)
    neg = jnp.float32(-jnp.inf)
    big = jnp.float32(LANES)
    l1 = jnp.where(lane < N_EXPERTS, logits, neg)
    m1 = jnp.max(l1, axis=-1, keepdims=True)
    i1 = jnp.min(jnp.where(l1 == m1, lanef, big), axis=-1, keepdims=True)
    l2 = jnp.where(lanef == i1, neg, l1)
    m2 = jnp.max(l2, axis=-1, keepdims=True)
    i2 = jnp.min(jnp.where(l2 == m2, lanef, big), axis=-1, keepdims=True)
    e = jnp.exp(m2 - m1)
    w1 = 1.0 / (1.0 + e)
    w2 = e / (1.0 + e)
    top = jnp.float32(N_EXPERTS - 1)
    i1, i2 = jnp.minimum(i1, top), jnp.minimum(i2, top)
    idx_ref[...] = jnp.where(lane == 0, i1, jnp.where(lane == 1, i2, 0.0)).astype(jnp.int32)
    wt_ref[...] = jnp.where(lane == 0, w1, jnp.where(lane == 1, w2, 0.0))


def _router(o, h, w_o, g, rw, rb, layer, tm):
    t, d = h.shape
    nb = o.shape[0]
    n = tm // nb
    return pl.pallas_call(
        _router_kernel,
        grid=(t // tm,),
        in_specs=[pl.BlockSpec((nb, n, d), lambda i: (0, i, 0)),
                  pl.BlockSpec((tm, d), lambda i: (i, 0)),
                  pl.BlockSpec((None, d, d), lambda i: (layer, 0, 0)),
                  pl.BlockSpec((None, 1, d), lambda i: (layer, 0, 0)),
                  pl.BlockSpec((d, 2 * LANES), lambda i: (0, 0)),
                  pl.BlockSpec((1, LANES), lambda i: (0, 0))],
        out_specs=[pl.BlockSpec((tm, d), lambda i: (i, 0)),
                   pl.BlockSpec((tm, d), lambda i: (i, 0)),
                   pl.BlockSpec((tm, LANES), lambda i: (i, 0)),
                   pl.BlockSpec((tm, LANES), lambda i: (i, 0))],
        out_shape=[jax.ShapeDtypeStruct((t, d), F32),
                   jax.ShapeDtypeStruct((t, d), F32),
                   jax.ShapeDtypeStruct((t, LANES), jnp.int32),
                   jax.ShapeDtypeStruct((t, LANES), F32)],
        scratch_shapes=[pltpu.VMEM((d // LANES, tm, LANES), F32)],
        compiler_params=_params("parallel"),
        name="router",
    )(o, h, w_o, g, rw, rb)


def _moe_kernel(te_ref, tv_ref, x_ref, wg_ref, wu_ref, wd_ref, out_ref, xb_ref):
    i = pl.program_id(0)
    f = pl.program_id(1)
    valid = tv_ref[i] > 0

    @pl.when(f == 0)
    def _():
        xb_ref[...] = x_ref[...].astype(BF16)
        out_ref[...] = jnp.zeros_like(out_ref)

    @pl.when(valid)
    def _():
        out_ref[...] += _swiglu_step(xb_ref[...], wg_ref, wu_ref, wd_ref)


def _moe_experts(xg, w_gu, w_d, tile_expert, tile_valid, j, tm):
    r, d = xg.shape
    ff = w_d.shape[2]
    tf = _ff_tile(ff)
    nf = ff // tf
    grid_spec = pltpu.PrefetchScalarGridSpec(
        num_scalar_prefetch=2,
        grid=(r // tm, nf),
        in_specs=[pl.BlockSpec((tm, d), lambda i, f, te, tv: (i, 0)),
                  pl.BlockSpec((None, None, d, tf), lambda i, f, te, tv: (j, te[i], 0, f)),
                  pl.BlockSpec((None, None, d, tf), lambda i, f, te, tv: (j, te[i], 0, nf + f)),
                  pl.BlockSpec((None, None, tf, d), lambda i, f, te, tv: (j, te[i], f, 0))],
        out_specs=pl.BlockSpec((tm, d), lambda i, f, te, tv: (i, 0)),
        scratch_shapes=[pltpu.VMEM((tm, d), BF16)])
    return pl.pallas_call(
        _moe_kernel,
        grid_spec=grid_spec,
        out_shape=jax.ShapeDtypeStruct((r, d), F32),
        compiler_params=_params("parallel", "arbitrary"),
        name="moe_experts",
    )(tile_expert, tile_valid, xg, w_gu, w_gu, w_d)


def _dispatch_tables(idx, tm):
    t = idx.shape[0]
    npairs = t * TOP_K
    ntiles = npairs // tm + N_EXPERTS
    e_flat = idx.reshape(npairs)
    onehot = (e_flat[:, None] == jnp.arange(N_EXPERTS, dtype=jnp.int32)[None, :]).astype(jnp.int32)
    csum = jnp.cumsum(onehot, axis=0)
    counts = csum[-1]
    rank = jnp.take_along_axis(csum, e_flat[:, None], axis=1)[:, 0] - 1
    padded = ((counts + tm - 1) // tm) * tm
    pend = jnp.cumsum(padded)
    pstart = pend - padded
    cstart = jnp.cumsum(counts) - counts
    pos = pstart[e_flat] + rank
    order = jnp.argsort(e_flat * npairs + jnp.arange(npairs, dtype=jnp.int32))
    tile_start = jnp.arange(ntiles, dtype=jnp.int32) * tm
    tile_expert = jnp.minimum(jnp.searchsorted(pend, tile_start, side="right"),
                              N_EXPERTS - 1).astype(jnp.int32)
    tile_valid = (tile_start < pend[-1]).astype(jnp.int32)
    slot = jnp.arange(ntiles * tm, dtype=jnp.int32)
    slot_e = jnp.repeat(tile_expert, tm)
    within = slot - pstart[slot_e]
    src = jnp.clip(cstart[slot_e] + within, 0, npairs - 1)
    row_token = jnp.where(within < counts[slot_e], order[src] // TOP_K, 0).astype(jnp.int32)
    return row_token, pos.reshape(t, TOP_K), tile_expert, tile_valid


def _final_kernel(h_ref, g_ref, out_ref, slab_ref):
    _batches_from_rows(_rms(h_ref[...], g_ref[...]), slab_ref, out_ref)


def _final_norm(h, g, nb, tm):
    t, d = h.shape
    return pl.pallas_call(
        _final_kernel,
        grid=(t // tm,),
        in_specs=[pl.BlockSpec((tm, d), lambda i: (i, 0)), pl.BlockSpec((1, d), lambda i: (0, 0))],
        out_specs=pl.BlockSpec((nb, tm // nb, d), lambda i: (0, i, 0)),
        out_shape=jax.ShapeDtypeStruct((nb, t // nb, d), F32),
        scratch_shapes=[pltpu.VMEM((d // LANES, tm, LANES), F32)],
        compiler_params=_params("parallel"),
        name="final_norm",
    )(h, g)


def kernel(x, mem, mem_norm_g, norm_mix_g, w_in, conv_w, conv_b, conv_ln_g, conv_ln_b, w_conv_pw,
           ssm_lambda_re, ssm_lambda_im, ssm_log_dt, ssm_b_re, ssm_b_im, ssm_c_re, ssm_c_im, ssm_d,
           w_ssm_glu, w_out, norm_xattn_g, w_q, w_kv, w_o, norm_ffn_g, ffn_w_gate_up, ffn_w_down,
           router_w, router_b, moe_w_gate_up, moe_w_down, final_norm_g):
    nb, s, d = x.shape
    assert nb == SUBLANES, "the row layout puts the batch on the sublane axis"
    depth = w_in.shape[0]
    cw = conv_w.shape[-1]
    sw = ssm_d.shape[1] * ssm_d.shape[2]
    t = nb * s
    tm = min(512, t)
    tm_ff = min(1024, t)
    ts = min(512, s)

    w_in_b = w_in.astype(BF16)
    w_pw_b = w_conv_pw.astype(BF16)
    w_glu_b = w_ssm_glu.astype(BF16)
    w_out_b = w_out.astype(BF16)
    w_q_b = w_q.astype(BF16)
    w_kv_b = w_kv.astype(BF16)
    w_o_b = w_o.astype(BF16)
    ffn_gu_b = ffn_w_gate_up.astype(BF16)
    ffn_d_b = ffn_w_down.astype(BF16)
    nslab = cw // LANES
    conv_w_rep = jnp.broadcast_to(conv_w.reshape(depth, CONV_KERNEL * nslab, 1, LANES),
                                  (depth, CONV_KERNEL * nslab, SUBLANES, LANES))
    conv_b_rep = jnp.broadcast_to(conv_b.reshape(depth, nslab, 1, LANES), (depth, nslab, SUBLANES, LANES))
    row3 = lambda a: a.reshape(a.shape[0], 1, a.shape[1])
    g_mix, g_x, g_ffn = row3(norm_mix_g), row3(norm_xattn_g), row3(norm_ffn_g)
    lg3, lb3 = row3(conv_ln_g), row3(conv_ln_b)
    rw_pad = jnp.pad(router_w, ((0, 0), (0, 0), (0, LANES - N_EXPERTS)))
    rw_hi = rw_pad.astype(BF16)
    rw_pad = jnp.concatenate([rw_hi, (rw_pad - rw_hi.astype(F32)).astype(BF16)], axis=-1)
    rb_pad = jnp.pad(router_b, ((0, 0), (0, LANES - N_EXPERTS)))

    kv = _kv_proj(mem.reshape(nb * mem.shape[1], d), mem_norm_g.reshape(1, d), w_kv_b)
    h = x

    for i in range(depth):
        u_conv, u_ssm, gates, *rows = _in_proj(h, g_mix, w_in_b, i, cw, sw, tm)
        if rows:
            h = rows[0]
        y_a = _conv_branch(u_conv, conv_w_rep, conv_b_rep, lg3, lb3, w_pw_b, i, tm)
        tables = _ssm_tables(ssm_lambda_re[i], ssm_lambda_im[i], ssm_log_dt[i], ssm_b_re[i], ssm_b_im[i],
                             ssm_c_re[i], ssm_c_im[i], ssm_d[i])
        y_s = _ssm_branch(u_ssm, *tables, rows=tm)
        h, q = _merge(y_s, w_glu_b, gates, y_a, h, w_out_b, g_x, w_q_b, i, nb, tm)
        o = _attention(q, kv, i, ts)
        j = i // 2
        if i % 2 == 0:
            h = _dense_ffn(o, h, w_o_b, g_ffn, ffn_gu_b, ffn_d_b, i, j, tm_ff)
        else:
            h, hn, idx, wts = _router(o, h, w_o_b, g_ffn, rw_pad[j], rb_pad[j:j + 1], i, tm)
            row_token, pos, tile_expert, tile_valid = _dispatch_tables(idx[:, :TOP_K], tm_ff)
            rows_of = lambda a, ix: a.at[ix].get(mode="promise_in_bounds")
            xg = rows_of(hn, row_token)
            yg = _moe_experts(xg, moe_w_gate_up, moe_w_down, tile_expert, tile_valid, j, tm_ff)
            h = h + wts[:, 0:1] * rows_of(yg, pos[:, 0]) + wts[:, 1:2] * rows_of(yg, pos[:, 1])

    return _final_norm(h, final_norm_g.reshape(1, d), nb, tm)
```

```python
import functools
import math

import jax
import jax.numpy as jnp
from jax import lax
from jax.experimental import pallas as pl
from jax.experimental.pallas import tpu as pltpu

F32 = jnp.float32
BF16 = jnp.bfloat16

RMS_EPS = 1e-6
LN_EPS = 1e-5
CONV_KERNEL = 31
SSM_GROUP = 16
SSM_STATE = 64
XATTN_HEADS = 4
N_EXPERTS = 8
TOP_K = 2

SUBLANES = 8
LANES = 128
HALO_BLOCK = 256
VMEM_LIMIT = 52 * 2 ** 20


def _params(*sem):
    return pltpu.CompilerParams(dimension_semantics=sem, vmem_limit_bytes=VMEM_LIMIT)


def _rms(x, g):
    return x * lax.rsqrt(jnp.mean(x * x, axis=-1, keepdims=True) + RMS_EPS) * g


def _dot(a, b):
    return jnp.dot(a, b, preferred_element_type=F32)


def _moe_combine(h_ref, y0_ref, y1_ref, wt_ref):
    return h_ref[...] + wt_ref[:, 0:1] * y0_ref[...] + wt_ref[:, 1:2] * y1_ref[...]


def _in_kernel(*refs, kind, cw, sw):
    if kind == "rows":
        h_ref, g_ref, w_ref, uc_ref, us_ref, gt_ref = refs
        x = h_ref[...]
    elif kind == "batches":
        h_ref, g_ref, w_ref, uc_ref, us_ref, gt_ref, hrow_ref, slab_ref = refs
        x = _rows_from_batches(h_ref, slab_ref)
        hrow_ref[...] = x
    else:
        h_ref, y0_ref, y1_ref, wt_ref, g_ref, w_ref, uc_ref, us_ref, gt_ref, hrow_ref = refs
        x = _moe_combine(h_ref, y0_ref, y1_ref, wt_ref)
        hrow_ref[...] = x
    xn = _rms(x, g_ref[...]).astype(BF16)
    a = _dot(xn, w_ref[:, 0:cw])
    gate = _dot(xn, w_ref[:, cw:2 * cw])
    uc_ref[...] = a * jax.nn.sigmoid(gate)
    us_ref[...] = _dot(xn, w_ref[:, 2 * cw:2 * cw + sw])
    gt_ref[...] = jax.nn.sigmoid(_dot(xn, w_ref[:, 2 * cw + sw:])).astype(BF16)


def _stream_specs(kind, stream, tm):
    d = stream[0].shape[-1]
    row = pl.BlockSpec((tm, d), lambda i: (i, 0))
    if kind == "rows":
        return [row]
    if kind == "batches":
        nb = stream[0].shape[0]
        return [pl.BlockSpec((nb, tm // nb, d), lambda i: (0, i, 0))]
    return [row, row, row, pl.BlockSpec((tm, LANES), lambda i: (i, 0))]


def _in_proj(kind, stream, g, w, layer, cw, sw, tm):
    d = stream[0].shape[-1]
    t = stream[0].size // d
    cols = w.shape[-1]
    ng = cols - 2 * cw - sw
    out_specs = [pl.BlockSpec((tm, cw), lambda i: (i, 0)),
                 pl.BlockSpec((tm, sw), lambda i: (i, 0)),
                 pl.BlockSpec((tm, ng), lambda i: (i, 0))]
    out_shape = [jax.ShapeDtypeStruct((t, cw), F32),
                 jax.ShapeDtypeStruct((t, sw), F32),
                 jax.ShapeDtypeStruct((t, ng), BF16)]
    scratch = []
    if kind != "rows":
        out_specs.append(pl.BlockSpec((tm, d), lambda i: (i, 0)))
        out_shape.append(jax.ShapeDtypeStruct((t, d), F32))
    if kind == "batches":
        scratch.append(pltpu.VMEM((d // LANES, tm, LANES), F32))
    return pl.pallas_call(
        functools.partial(_in_kernel, kind=kind, cw=cw, sw=sw),
        grid=(t // tm,),
        in_specs=_stream_specs(kind, stream, tm) + [
            pl.BlockSpec((None, 1, d), lambda i: (layer, 0, 0)),
            pl.BlockSpec((None, d, cols), lambda i: (layer, 0, 0))],
        out_specs=out_specs,
        out_shape=out_shape,
        scratch_shapes=scratch,
        compiler_params=_params("parallel"),
        name="in_proj",
    )(*stream, g, w)


def _conv_kernel(cur_ref, prev_ref, cw_ref, cb_ref, lg_ref, lb_ref, wpw_ref, out_ref,
                 ext_ref, conv_ref, act_ref, *, rows_per_chunk):
    i = pl.program_id(0)
    tm, c = cur_ref.shape
    nslab = c // LANES
    prev = jnp.where(i > 0, prev_ref[...], 0.0)
    for j in range(nslab):
        ext_ref[j, 0:HALO_BLOCK, :] = prev[:, j * LANES:(j + 1) * LANES]
        ext_ref[j, HALO_BLOCK:, :] = cur_ref[:, j * LANES:(j + 1) * LANES]
    first = HALO_BLOCK - (CONV_KERNEL - 1) * SUBLANES
    rc = rows_per_chunk
    rcb = rc
    reps = rcb // SUBLANES

    def conv_block(n, carry):
        ci = n // nslab
        j = n % nslab
        r0 = pl.multiple_of(ci * rcb, rcb)
        accs = [cb_ref[j]] * reps
        for k in range(CONV_KERNEL):
            w = cw_ref[k * nslab + j]
            accs = [a + w * ext_ref[j, pl.ds(r0 + first + SUBLANES * (k + r), SUBLANES), :]
                    for r, a in enumerate(accs)]
        conv_ref[j, pl.ds(r0, rcb), :] = jnp.concatenate(accs, axis=0)
        return carry

    lax.fori_loop(0, (tm // rcb) * nslab, conv_block, 0)

    def norm_chunk(ci, carry):
        r0 = pl.multiple_of(ci * rc, rc)
        acc = jnp.concatenate([conv_ref[j, pl.ds(r0, rc), :] for j in range(nslab)], axis=-1)
        mu = jnp.mean(acc, axis=-1, keepdims=True)
        xc = acc - mu
        y = xc * lax.rsqrt(jnp.mean(xc * xc, axis=-1, keepdims=True) + LN_EPS)
        y = y * lg_ref[...] + lb_ref[...]
        act_ref[pl.ds(r0, rc), :] = jax.nn.silu(y).astype(BF16)
        return carry

    lax.fori_loop(0, tm // rc, norm_chunk, 0, unroll=4)
    out_ref[...] = _dot(act_ref[...], wpw_ref[...]).astype(BF16)


def _conv_branch(u, conv_w, conv_b, ln_g, ln_b, w_pw, layer, tm):
    t, c = u.shape
    d = w_pw.shape[-1]
    ratio = tm // HALO_BLOCK
    return pl.pallas_call(
        functools.partial(_conv_kernel, rows_per_chunk=64),
        grid=(t // tm,),
        in_specs=[pl.BlockSpec((tm, c), lambda i: (i, 0)),
                  pl.BlockSpec((HALO_BLOCK, c), lambda i: (jnp.maximum(i * ratio - 1, 0), 0)),
                  pl.BlockSpec((None,) + conv_w.shape[1:], lambda i: (layer, 0, 0, 0)),
                  pl.BlockSpec((None,) + conv_b.shape[1:], lambda i: (layer, 0, 0, 0)),
                  pl.BlockSpec((None, 1, c), lambda i: (layer, 0, 0)),
                  pl.BlockSpec((None, 1, c), lambda i: (layer, 0, 0)),
                  pl.BlockSpec((None, c, d), lambda i: (layer, 0, 0))],
        out_specs=pl.BlockSpec((tm, d), lambda i: (i, 0)),
        out_shape=jax.ShapeDtypeStruct((t, d), BF16),
        scratch_shapes=[pltpu.VMEM((c // LANES, tm + HALO_BLOCK, LANES), F32),
                        pltpu.VMEM((c // LANES, tm, LANES), F32), pltpu.VMEM((tm, c), BF16)],
        compiler_params=_params("parallel"),
        name="conv_branch",
    )(u, u, conv_w, conv_b, ln_g, ln_b, w_pw)


def _ssm_kernel(u_ref, bm_ref, cm_ref, a_ref, d_ref, out_ref, hs_ref, st_ref, *, nsplit):
    step_idx = pl.program_id(0)
    rows, width = u_ref.shape
    hw = width // 2
    hs_cols = hs_ref.shape[1] // 2
    hstates = hs_cols // 2
    sub = rows // nsplit

    @pl.when(step_idx == 0)
    def _():
        st_ref[...] = jnp.zeros_like(st_ref)

    cols = [(hf * hs_cols + j * LANES, hf * hs_cols + hstates + j * LANES, hf * hstates + j * LANES)
            for hf in range(2) for j in range(hstates // LANES)]

    def input_matmul(p):
        part = slice(p * sub, (p + 1) * sub)
        for hf in range(2):
            ub = u_ref[part, hf * hw:(hf + 1) * hw].astype(BF16)
            hs_ref[part, hf * hs_cols:(hf + 1) * hs_cols] = _dot(ub, bm_ref[hf])

    def scan(p):
        for cr, ci, ac in cols:
            ar = a_ref[0:SUBLANES, ac:ac + LANES]
            ai = a_ref[SUBLANES:2 * SUBLANES, ac:ac + LANES]
            hr = st_ref[:, cr:cr + LANES]
            hi = st_ref[:, ci:ci + LANES]
            for t in range(sub // SUBLANES):
                step = slice(p * sub + t * SUBLANES, p * sub + (t + 1) * SUBLANES)
                bur = hs_ref[step, cr:cr + LANES]
                bui = hs_ref[step, ci:ci + LANES]
                hr, hi = ar * hr - ai * hi + bur, ar * hi + ai * hr + bui
                hs_ref[step, cr:cr + LANES] = hr
                hs_ref[step, ci:ci + LANES] = hi
            st_ref[:, cr:cr + LANES] = hr
            st_ref[:, ci:ci + LANES] = hi

    def output_matmul(p):
        part = slice(p * sub, (p + 1) * sub)
        for hf in range(2):
            half = slice(hf * hw, (hf + 1) * hw)
            hb = hs_ref[part, hf * hs_cols:(hf + 1) * hs_cols].astype(BF16)
            y = _dot(hb, cm_ref[hf]) + d_ref[:, half] * u_ref[part, half]
            out_ref[part, half] = jax.nn.gelu(y).astype(BF16)

    input_matmul(0)
    for p in range(nsplit):
        if p + 1 < nsplit:
            input_matmul(p + 1)
        scan(p)
        output_matmul(p)


def _ssm_branch(u, bm, cm, a_rows, d_row, rows):
    t, width = u.shape
    ncols = 2 * bm.shape[-1]
    return pl.pallas_call(
        functools.partial(_ssm_kernel, nsplit=2),
        grid=(t // rows,),
        in_specs=[pl.BlockSpec((rows, width), lambda i: (i, 0)),
                  pl.BlockSpec(bm.shape, lambda i: (0, 0, 0)),
                  pl.BlockSpec(cm.shape, lambda i: (0, 0, 0)),
                  pl.BlockSpec(a_rows.shape, lambda i: (0, 0)),
                  pl.BlockSpec(d_row.shape, lambda i: (0, 0))],
        out_specs=pl.BlockSpec((rows, width), lambda i: (i, 0)),
        out_shape=jax.ShapeDtypeStruct((t, width), BF16),
        scratch_shapes=[pltpu.VMEM((rows, ncols), F32), pltpu.VMEM((SUBLANES, ncols), F32)],
        compiler_params=_params("arbitrary"),
        name="ssm_branch",
    )(u, bm, cm, a_rows, d_row)


def _ssm_tables(lam_re, lam_im, log_dt, b_re, b_im, c_re, c_im, d_skip):
    g, p = lam_re.shape
    gh = g // 2
    dt = jnp.exp(log_dt)[:, None]
    mag = jnp.exp(lam_re * dt)
    ar = mag * jnp.cos(lam_im * dt)
    ai = mag * jnp.sin(lam_im * dt)
    den = lam_re * lam_re + lam_im * lam_im
    xr = ar - 1.0
    kr = (xr * lam_re + ai * lam_im) / den
    ki = (ai * lam_re - xr * lam_im) / den
    bbar_r = kr[..., None] * b_re - ki[..., None] * b_im
    bbar_i = kr[..., None] * b_im + ki[..., None] * b_re
    eye = jnp.eye(gh, dtype=F32)

    def bd_in(m):
        m = m.reshape(2, gh, p, SSM_GROUP)
        return jnp.einsum("aqph,qr->aqhrp", m, eye).reshape(2, gh * SSM_GROUP, gh * p)

    def bd_out(m):
        m = m.reshape(2, gh, SSM_GROUP, p)
        return jnp.einsum("aqhp,qr->aqprh", m, eye).reshape(2, gh * p, gh * SSM_GROUP)

    bm = jnp.concatenate([bd_in(bbar_r), bd_in(bbar_i)], axis=-1).astype(BF16)
    cm = jnp.concatenate([bd_out(c_re), -bd_out(c_im)], axis=1).astype(BF16)
    a_rows = jnp.concatenate([jnp.broadcast_to(ar.reshape(1, g * p), (SUBLANES, g * p)),
                              jnp.broadcast_to(ai.reshape(1, g * p), (SUBLANES, g * p))], axis=0)
    return bm, cm, a_rows, d_skip.reshape(1, g * SSM_GROUP)


def _batches_from_rows(val, slab_ref, out_ref):
    nb, n, d = out_ref.shape
    nslab = d // LANES
    for j in range(nslab):
        slab_ref[j] = val[:, j * LANES:(j + 1) * LANES]
    for b in range(nb):
        rows = [slab_ref[j, pl.ds(b, n, stride=nb), :] for j in range(nslab)]
        out_ref[b] = jnp.concatenate(rows, axis=-1).astype(out_ref.dtype)


def _rows_from_batches(in_ref, slab_ref):
    nb, n, d = in_ref.shape
    nslab = d // LANES
    for b in range(nb):
        xb = in_ref[b].astype(F32)
        for j in range(nslab):
            slab_ref[j, pl.ds(b, n, stride=nb), :] = xb[:, j * LANES:(j + 1) * LANES]
    return jnp.concatenate([slab_ref[j] for j in range(nslab)], axis=-1)


def _merge_kernel(ys_ref, wglu_ref, gt_ref, ya_ref, h_ref, wout_ref, gx_ref, wq_ref, hout_ref, q_ref,
                  slab_ref):
    d = h_ref.shape[1]
    z = _dot(ys_ref[...], wglu_ref[...])
    yb = z[:, :d] * jax.nn.sigmoid(z[:, d:])
    m = gt_ref[:, :d].astype(F32) * ya_ref[...].astype(F32) + gt_ref[:, d:].astype(F32) * yb
    h2 = h_ref[...] + _dot(m.astype(BF16), wout_ref[...])
    hout_ref[...] = h2
    _batches_from_rows(_dot(_rms(h2, gx_ref[...]).astype(BF16), wq_ref[...]), slab_ref, q_ref)


def _merge(ys, w_glu, gates, ya, h, w_out, gx, w_q, layer, nb, tm):
    t, d = h.shape
    sw = ys.shape[1]
    n = tm // nb
    return pl.pallas_call(
        _merge_kernel,
        grid=(t // tm,),
        in_specs=[pl.BlockSpec((tm, sw), lambda i: (i, 0)),
                  pl.BlockSpec((None, sw, 2 * d), lambda i: (layer, 0, 0)),
                  pl.BlockSpec((tm, 2 * d), lambda i: (i, 0)),
                  pl.BlockSpec((tm, d), lambda i: (i, 0)),
                  pl.BlockSpec((tm, d), lambda i: (i, 0)),
                  pl.BlockSpec((None, d, d), lambda i: (layer, 0, 0)),
                  pl.BlockSpec((None, 1, d), lambda i: (layer, 0, 0)),
                  pl.BlockSpec((None, d, d), lambda i: (layer, 0, 0))],
        out_specs=[pl.BlockSpec((tm, d), lambda i: (i, 0)),
                   pl.BlockSpec((nb, n, d), lambda i: (0, i, 0))],
        out_shape=[jax.ShapeDtypeStruct((t, d), F32), jax.ShapeDtypeStruct((nb, t // nb, d), BF16)],
        scratch_shapes=[pltpu.VMEM((d // LANES, tm, LANES), F32)],
        compiler_params=_params("parallel"),
        name="merge",
    )(ys, w_glu, gates, ya, h, w_out, gx, w_q)


def _kv_kernel(mem_ref, g_ref, w_ref, out_ref):
    xn = _rms(mem_ref[...], g_ref[...]).astype(BF16)
    out_ref[...] = _dot(xn, w_ref[...]).astype(BF16)


def _kv_proj(mem2d, g, w_kv):
    nl, d, d2 = w_kv.shape
    bm = mem2d.shape[0]
    return pl.pallas_call(
        _kv_kernel,
        grid=(nl, d2 // d),
        in_specs=[pl.BlockSpec((bm, d), lambda l, j: (0, 0)),
                  pl.BlockSpec((1, d), lambda l, j: (0, 0)),
                  pl.BlockSpec((None, d, d), lambda l, j: (l, 0, j))],
        out_specs=pl.BlockSpec((None, bm, d), lambda l, j: (l, 0, j)),
        out_shape=jax.ShapeDtypeStruct((nl, bm, d2), BF16),
        compiler_params=_params("parallel", "parallel"),
        name="kv_proj",
    )(mem2d, g, w_kv)


def _attn_kernel(q_ref, kv_ref, out_ref):
    d = q_ref.shape[1]
    hd = d // XATTN_HEADS
    scale = 1.0 / math.sqrt(hd)
    for n in range(XATTN_HEADS):
        qh = q_ref[:, n * hd:(n + 1) * hd]
        kh = kv_ref[:, n * hd:(n + 1) * hd]
        vh = kv_ref[:, d + n * hd:d + (n + 1) * hd]
        s = lax.dot_general(qh, kh, (((1,), (1,)), ((), ())), preferred_element_type=F32) * scale
        e = jnp.exp(s - jnp.max(s, axis=-1, keepdims=True))
        p = e / jnp.sum(e, axis=-1, keepdims=True)
        out_ref[:, n * hd:(n + 1) * hd] = _dot(p.astype(BF16), vh).astype(out_ref.dtype)


def _attention(q, kv, layer, ts):
    nb, s, d = q.shape
    m = kv.shape[1] // nb
    return pl.pallas_call(
        _attn_kernel,
        grid=(nb, s // ts),
        in_specs=[pl.BlockSpec((None, ts, d), lambda b, i: (b, i, 0)),
                  pl.BlockSpec((None, m, 2 * d), lambda b, i: (layer, b, 0))],
        out_specs=pl.BlockSpec((None, ts, d), lambda b, i: (b, i, 0)),
        out_shape=jax.ShapeDtypeStruct((nb, s, d), BF16),
        compiler_params=_params("parallel", "parallel"),
        name="xattn",
    )(q, kv)


def _attn_residual(o_ref, h_ref, wo_ref, slab_ref):
    o = _rows_from_batches(o_ref, slab_ref).astype(BF16)
    return h_ref[...] + _dot(o, wo_ref[...])


def _swiglu_step(xn, wg_ref, wu_ref, wd_ref):
    g = _dot(xn, wg_ref[...].astype(BF16))
    u = _dot(xn, wu_ref[...].astype(BF16))
    return _dot((jax.nn.silu(g) * u).astype(BF16), wd_ref[...].astype(BF16))


def _ffn_kernel(o_ref, h_ref, wo_ref, g_ref, wg_ref, wu_ref, wd_ref, out_ref, slab_ref, xn_ref):
    @pl.when(pl.program_id(1) == 0)
    def _():
        h2 = _attn_residual(o_ref, h_ref, wo_ref, slab_ref)
        xn_ref[...] = _rms(h2, g_ref[...]).astype(BF16)
        out_ref[...] = h2

    out_ref[...] += _swiglu_step(xn_ref[...], wg_ref, wu_ref, wd_ref)


def _ff_tile(ff):
    for tf in (512, 256, 128):
        if ff % tf == 0:
            return tf
    raise ValueError(f"hidden size {ff} is not a multiple of {LANES}")


def _dense_ffn(o, h, w_o, g, w_gu, w_d, layer, j, tm):
    t, d = h.shape
    nb = o.shape[0]
    n = tm // nb
    ff = w_d.shape[1]
    tf = _ff_tile(ff)
    nf = ff // tf
    return pl.pallas_call(
        _ffn_kernel,
        grid=(t // tm, nf),
        in_specs=[pl.BlockSpec((nb, n, d), lambda i, f: (0, i, 0)),
                  pl.BlockSpec((tm, d), lambda i, f: (i, 0)),
                  pl.BlockSpec((None, d, d), lambda i, f: (layer, 0, 0), pipeline_mode=pl.Buffered(1)),
                  pl.BlockSpec((None, 1, d), lambda i, f: (layer, 0, 0)),
                  pl.BlockSpec((None, d, tf), lambda i, f: (j, 0, f)),
                  pl.BlockSpec((None, d, tf), lambda i, f: (j, 0, nf + f)),
                  pl.BlockSpec((None, tf, d), lambda i, f: (j, f, 0))],
        out_specs=pl.BlockSpec((tm, d), lambda i, f: (i, 0)),
        out_shape=jax.ShapeDtypeStruct((t, d), F32),
        scratch_shapes=[pltpu.VMEM((d // LANES, tm, LANES), F32), pltpu.VMEM((tm, d), BF16)],
        compiler_params=_params("parallel", "arbitrary"),
        name="dense_ffn",
    )(o, h, w_o, g, w_gu, w_gu, w_d)


def _router_kernel(o_ref, h_ref, wo_ref, g_ref, rw_ref, rb_ref, h2_ref, hn_ref, idx_ref, wt_ref, slab_ref):
    h2 = _attn_residual(o_ref, h_ref, wo_ref, slab_ref)
    h2_ref[...] = h2
    xn = _rms(h2, g_ref[...])
    xh = xn.astype(BF16)
    hn_ref[...] = xn
    xl = (xn - xh.astype(F32)).astype(BF16)
    ph = _dot(xh, rw_ref[...])
    logits = ph[:, :LANES] + ph[:, LANES:] + _dot(xl, rw_ref[:, :LANES]) + rb_ref[...]
    lane = lax.broadcasted_iota(jnp.int32, logits.shape, 1)
    lanef = lane.astype(F32)
    neg = jnp.float32(-jnp.inf)
    big = jnp.float32(LANES)
    l1 = jnp.where(lane < N_EXPERTS, logits, neg)
    m1 = jnp.max(l1, axis=-1, keepdims=True)
    i1 = jnp.min(jnp.where(l1 == m1, lanef, big), axis=-1, keepdims=True)
    l2 = jnp.where(lanef == i1, neg, l1)
    m2 = jnp.max(l2, axis=-1, keepdims=True)
    i2 = jnp.min(jnp.where(l2 == m2, lanef, big), axis=-1, keepdims=True)
    e = jnp.exp(m2 - m1)
    w1 = 1.0 / (1.0 + e)
    w2 = e / (1.0 + e)
    top = jnp.float32(N_EXPERTS - 1)
    i1, i2 = jnp.minimum(i1, top), jnp.minimum(i2, top)
    idx_ref[...] = jnp.where(lane == 0, i1, jnp.where(lane == 1, i2, 0.0)).astype(jnp.int32)
    wt_ref[...] = jnp.where(lane == 0, w1, jnp.where(lane == 1, w2, 0.0))


def _router(o, h, w_o, g, rw, rb, layer, tm):
    t, d = h.shape
    nb = o.shape[0]
    n = tm // nb
    return pl.pallas_call(
        _router_kernel,
        grid=(t // tm,),
        in_specs=[pl.BlockSpec((nb, n, d), lambda i: (0, i, 0)),
                  pl.BlockSpec((tm, d), lambda i: (i, 0)),
                  pl.BlockSpec((None, d, d), lambda i: (layer, 0, 0)),
                  pl.BlockSpec((None, 1, d), lambda i: (layer, 0, 0)),
                  pl.BlockSpec((d, 2 * LANES), lambda i: (0, 0)),
                  pl.BlockSpec((1, LANES), lambda i: (0, 0))],
        out_specs=[pl.BlockSpec((tm, d), lambda i: (i, 0)),
                   pl.BlockSpec((tm, d), lambda i: (i, 0)),
                   pl.BlockSpec((tm, LANES), lambda i: (i, 0)),
                   pl.BlockSpec((tm, LANES), lambda i: (i, 0))],
        out_shape=[jax.ShapeDtypeStruct((t, d), F32),
                   jax.ShapeDtypeStruct((t, d), F32),
                   jax.ShapeDtypeStruct((t, LANES), jnp.int32),
                   jax.ShapeDtypeStruct((t, LANES), F32)],
        scratch_shapes=[pltpu.VMEM((d // LANES, tm, LANES), F32)],
        compiler_params=_params("parallel"),
        name="router",
    )(o, h, w_o, g, rw, rb)


def _moe_kernel(te_ref, tv_ref, x_ref, wg_ref, wu_ref, wd_ref, *rest, tile0):
    out_ref, xb_ref = rest[-2:]
    i = tile0 + pl.program_id(0)
    f = pl.program_id(1)
    valid = tv_ref[i] > 0

    @pl.when(f == 0)
    def _():
        xb_ref[...] = x_ref[...].astype(BF16)
        out_ref[...] = jnp.zeros_like(out_ref)

    @pl.when(valid)
    def _():
        out_ref[...] += _swiglu_step(xb_ref[...], wg_ref, wu_ref, wd_ref)


def _moe_experts(xg, w_gu, w_d, tile_expert, tile_valid, j, tm, tile0, total_rows, prev):
    r, d = xg.shape
    ff = w_d.shape[2]
    tf = _ff_tile(ff)
    nf = ff // tf
    in_specs = [pl.BlockSpec((tm, d), lambda i, f, te, tv: (i, 0)),
                pl.BlockSpec((None, None, d, tf), lambda i, f, te, tv: (j, te[tile0 + i], 0, f)),
                pl.BlockSpec((None, None, d, tf), lambda i, f, te, tv: (j, te[tile0 + i], 0, nf + f)),
                pl.BlockSpec((None, None, tf, d), lambda i, f, te, tv: (j, te[tile0 + i], f, 0))]
    args = [tile_expert, tile_valid, xg, w_gu, w_gu, w_d]
    aliases = {}
    if prev is not None:
        in_specs.append(pl.BlockSpec(memory_space=pl.ANY))
        aliases = {len(args): 0}
        args.append(prev)
    grid_spec = pltpu.PrefetchScalarGridSpec(
        num_scalar_prefetch=2,
        grid=(r // tm, nf),
        in_specs=in_specs,
        out_specs=pl.BlockSpec((tm, d), lambda i, f, te, tv: (tile0 + i, 0)),
        scratch_shapes=[pltpu.VMEM((tm, d), BF16)])
    return pl.pallas_call(
        functools.partial(_moe_kernel, tile0=tile0),
        grid_spec=grid_spec,
        out_shape=jax.ShapeDtypeStruct((total_rows, d), F32),
        input_output_aliases=aliases,
        compiler_params=_params("parallel", "arbitrary"),
        name="moe_experts",
    )(*args)


def _dispatch_tables(idx, tm):
    t = idx.shape[0]
    npairs = t * TOP_K
    ntiles = npairs // tm + N_EXPERTS
    e_flat = idx.reshape(npairs)
    onehot = (e_flat[:, None] == jnp.arange(N_EXPERTS, dtype=jnp.int32)[None, :]).astype(jnp.int32)
    csum = jnp.cumsum(onehot, axis=0)
    counts = csum[-1]
    rank = jnp.take_along_axis(csum, e_flat[:, None], axis=1)[:, 0] - 1
    padded = ((counts + tm - 1) // tm) * tm
    pend = jnp.cumsum(padded)
    pstart = pend - padded
    cstart = jnp.cumsum(counts) - counts
    pos = pstart[e_flat] + rank
    order = jnp.argsort(e_flat * npairs + jnp.arange(npairs, dtype=jnp.int32))
    tile_start = jnp.arange(ntiles, dtype=jnp.int32) * tm
    tile_expert = jnp.minimum(jnp.searchsorted(pend, tile_start, side="right"),
                              N_EXPERTS - 1).astype(jnp.int32)
    tile_valid = (tile_start < pend[-1]).astype(jnp.int32)
    slot = jnp.arange(ntiles * tm, dtype=jnp.int32)
    slot_e = jnp.repeat(tile_expert, tm)
    within = slot - pstart[slot_e]
    src = jnp.clip(cstart[slot_e] + within, 0, npairs - 1)
    row_token = jnp.where(within < counts[slot_e], order[src] // TOP_K, slot % t).astype(jnp.int32)
    return row_token, pos.reshape(t, TOP_K), tile_expert, tile_valid


def _final_kernel(*refs, kind):
    *stream, g_ref, out_ref, slab_ref = refs
    x = stream[0][...] if kind == "rows" else _moe_combine(*stream)
    _batches_from_rows(_rms(x, g_ref[...]), slab_ref, out_ref)


def _final_norm(kind, stream, g, nb, tm):
    t, d = stream[0].shape
    return pl.pallas_call(
        functools.partial(_final_kernel, kind=kind),
        grid=(t // tm,),
        in_specs=_stream_specs(kind, stream, tm) + [pl.BlockSpec((1, d), lambda i: (0, 0))],
        out_specs=pl.BlockSpec((nb, tm // nb, d), lambda i: (0, i, 0)),
        out_shape=jax.ShapeDtypeStruct((nb, t // nb, d), F32),
        scratch_shapes=[pltpu.VMEM((d // LANES, tm, LANES), F32)],
        compiler_params=_params("parallel"),
        name="final_norm",
    )(*stream, g)


def kernel(x, mem, mem_norm_g, norm_mix_g, w_in, conv_w, conv_b, conv_ln_g, conv_ln_b, w_conv_pw,
           ssm_lambda_re, ssm_lambda_im, ssm_log_dt, ssm_b_re, ssm_b_im, ssm_c_re, ssm_c_im, ssm_d,
           w_ssm_glu, w_out, norm_xattn_g, w_q, w_kv, w_o, norm_ffn_g, ffn_w_gate_up, ffn_w_down,
           router_w, router_b, moe_w_gate_up, moe_w_down, final_norm_g):
    nb, s, d = x.shape
    assert nb == SUBLANES, "the row layout puts the batch on the sublane axis"
    depth = w_in.shape[0]
    cw = conv_w.shape[-1]
    sw = ssm_d.shape[1] * ssm_d.shape[2]
    t = nb * s
    tm = min(512, t)
    tm_ff = min(1024, t)
    ts = min(512, s)

    w_in_b = w_in.astype(BF16)
    w_pw_b = w_conv_pw.astype(BF16)
    w_glu_b = w_ssm_glu.astype(BF16)
    w_out_b = w_out.astype(BF16)
    w_q_b = w_q.astype(BF16)
    w_kv_b = w_kv.astype(BF16)
    w_o_b = w_o.astype(BF16)
    ffn_gu_b = ffn_w_gate_up.astype(BF16)
    ffn_d_b = ffn_w_down.astype(BF16)
    nslab = cw // LANES
    conv_w_rep = jnp.broadcast_to(conv_w.reshape(depth, CONV_KERNEL * nslab, 1, LANES),
                                  (depth, CONV_KERNEL * nslab, SUBLANES, LANES))
    conv_b_rep = jnp.broadcast_to(conv_b.reshape(depth, nslab, 1, LANES), (depth, nslab, SUBLANES, LANES))
    row3 = lambda a: a.reshape(a.shape[0], 1, a.shape[1])
    g_mix, g_x, g_ffn = row3(norm_mix_g), row3(norm_xattn_g), row3(norm_ffn_g)
    lg3, lb3 = row3(conv_ln_g), row3(conv_ln_b)
    rw_pad = jnp.pad(router_w, ((0, 0), (0, 0), (0, LANES - N_EXPERTS)))
    rw_hi = rw_pad.astype(BF16)
    rw_pad = jnp.concatenate([rw_hi, (rw_pad - rw_hi.astype(F32)).astype(BF16)], axis=-1)
    rb_pad = jnp.pad(router_b, ((0, 0), (0, LANES - N_EXPERTS)))

    kv = _kv_proj(mem.reshape(nb * mem.shape[1], d), mem_norm_g.reshape(1, d), w_kv_b)
    kind, stream = "batches", (x,)

    for i in range(depth):
        u_conv, u_ssm, gates, *rows = _in_proj(kind, stream, g_mix, w_in_b, i, cw, sw, tm)
        h = rows[0] if rows else stream[0]
        y_a = _conv_branch(u_conv, conv_w_rep, conv_b_rep, lg3, lb3, w_pw_b, i, tm)
        tables = _ssm_tables(ssm_lambda_re[i], ssm_lambda_im[i], ssm_log_dt[i], ssm_b_re[i], ssm_b_im[i],
                             ssm_c_re[i], ssm_c_im[i], ssm_d[i])
        y_s = _ssm_branch(u_ssm, *tables, rows=tm)
        h, q = _merge(y_s, w_glu_b, gates, y_a, h, w_out_b, g_x, w_q_b, i, nb, tm)
        o = _attention(q, kv, i, ts)
        j = i // 2
        if i % 2 == 0:
            h = _dense_ffn(o, h, w_o_b, g_ffn, ffn_gu_b, ffn_d_b, i, j, tm_ff)
            kind, stream = "rows", (h,)
        else:
            h, hn, idx, wts = _router(o, h, w_o_b, g_ffn, rw_pad[j], rb_pad[j:j + 1], i, tm)
            row_token, pos, tile_expert, tile_valid = _dispatch_tables(idx[:, :TOP_K], tm_ff)
            rows_of = lambda a, ix: a.at[ix].get(mode="promise_in_bounds")
            ntiles = tile_expert.shape[0]
            ngroups = next(c for c in (4, 3, 2, 1) if ntiles % c == 0)
            per = ntiles // ngroups
            yg = None
            for c in range(ngroups):
                xg = rows_of(hn, row_token[c * per * tm_ff:(c + 1) * per * tm_ff])
                yg = _moe_experts(xg, moe_w_gate_up, moe_w_down, tile_expert, tile_valid, j, tm_ff,
                                  c * per, ntiles * tm_ff, yg)
            kind, stream = "moe", (h, rows_of(yg, pos[:, 0]), rows_of(yg, pos[:, 1]), wts)

    return _final_norm(kind, stream, final_norm_g.reshape(1, d), nb, tm)
```

```python
import functools
import math

import jax
import jax.numpy as jnp
from jax import lax
from jax.experimental import pallas as pl
from jax.experimental.pallas import tpu as pltpu

F32 = jnp.float32
BF16 = jnp.bfloat16

RMS_EPS = 1e-6
LN_EPS = 1e-5
CONV_KERNEL = 31
SSM_GROUP = 16
SSM_STATE = 64
XATTN_HEADS = 4
N_EXPERTS = 8
TOP_K = 2

SUBLANES = 8
LANES = 128
HALO_BLOCK = 256
ROW_PART = 256
VMEM_LIMIT = 52 * 2 ** 20


def _params(*sem):
    return pltpu.CompilerParams(dimension_semantics=sem, vmem_limit_bytes=VMEM_LIMIT)


def _rms(x, g):
    return x * lax.rsqrt(jnp.mean(x * x, axis=-1, keepdims=True) + RMS_EPS) * g


def _dot(a, b):
    return jnp.dot(a, b, preferred_element_type=F32)


def _moe_combine(h_ref, y0_ref, y1_ref, wt_ref, rows=slice(None)):
    return h_ref[rows, :] + wt_ref[rows, 0:1] * y0_ref[rows, :] + wt_ref[rows, 1:2] * y1_ref[rows, :]


def _row_parts(tm, nsplit):
    sub = tm // nsplit
    return [slice(p * sub, (p + 1) * sub) for p in range(nsplit)]


def _in_kernel(*refs, kind, cw, sw, nsplit):
    if kind == "rows":
        h_ref, g_ref, w_ref, uc_ref, us_ref, gt_ref = refs
        x_of = lambda r: h_ref[r, :]
    elif kind == "batches":
        h_ref, g_ref, w_ref, uc_ref, us_ref, gt_ref, hrow_ref, slab_ref = refs
        hrow_ref[...] = _rows_from_batches(h_ref, slab_ref)
        x_of = lambda r: hrow_ref[r, :]
    else:
        h_ref, y0_ref, y1_ref, wt_ref, g_ref, w_ref, uc_ref, us_ref, gt_ref, hrow_ref = refs

        def x_of(r):
            x = _moe_combine(h_ref, y0_ref, y1_ref, wt_ref, r)
            hrow_ref[r, :] = x
            return x

    for r in _row_parts(uc_ref.shape[0], nsplit):
        xn = _rms(x_of(r), g_ref[...]).astype(BF16)
        a = _dot(xn, w_ref[:, 0:cw])
        gate = _dot(xn, w_ref[:, cw:2 * cw])
        uc_ref[r, :] = a * jax.nn.sigmoid(gate)
        us_ref[r, :] = _dot(xn, w_ref[:, 2 * cw:2 * cw + sw])
        gt_ref[r, :] = jax.nn.sigmoid(_dot(xn, w_ref[:, 2 * cw + sw:])).astype(BF16)


def _stream_specs(kind, stream, tm):
    d = stream[0].shape[-1]
    row = pl.BlockSpec((tm, d), lambda i: (i, 0))
    if kind == "rows":
        return [row]
    if kind == "batches":
        nb = stream[0].shape[0]
        return [pl.BlockSpec((nb, tm // nb, d), lambda i: (0, i, 0))]
    return [row, row, row, pl.BlockSpec((tm, LANES), lambda i: (i, 0))]


def _in_proj(kind, stream, g, w, layer, cw, sw, tm):
    d = stream[0].shape[-1]
    t = stream[0].size // d
    cols = w.shape[-1]
    ng = cols - 2 * cw - sw
    out_specs = [pl.BlockSpec((tm, cw), lambda i: (i, 0)),
                 pl.BlockSpec((tm, sw), lambda i: (i, 0)),
                 pl.BlockSpec((tm, ng), lambda i: (i, 0))]
    out_shape = [jax.ShapeDtypeStruct((t, cw), F32),
                 jax.ShapeDtypeStruct((t, sw), F32),
                 jax.ShapeDtypeStruct((t, ng), BF16)]
    scratch = []
    if kind != "rows":
        out_specs.append(pl.BlockSpec((tm, d), lambda i: (i, 0)))
        out_shape.append(jax.ShapeDtypeStruct((t, d), F32))
    if kind == "batches":
        scratch.append(pltpu.VMEM((d // LANES, tm, LANES), F32))
    return pl.pallas_call(
        functools.partial(_in_kernel, kind=kind, cw=cw, sw=sw, nsplit=tm // ROW_PART),
        grid=(t // tm,),
        in_specs=_stream_specs(kind, stream, tm) + [
            pl.BlockSpec((None, 1, d), lambda i: (layer, 0, 0)),
            pl.BlockSpec((None, d, cols), lambda i: (layer, 0, 0), pipeline_mode=pl.Buffered(1))],
        out_specs=out_specs,
        out_shape=out_shape,
        scratch_shapes=scratch,
        compiler_params=_params("parallel"),
        name="in_proj",
    )(*stream, g, w)


def _conv_kernel(cur_ref, prev_ref, cw_ref, cb_ref, lg_ref, lb_ref, wpw_ref, out_ref,
                 ext_ref, conv_ref, act_ref, *, rows_per_chunk):
    i = pl.program_id(0)
    tm, c = cur_ref.shape
    nslab = c // LANES
    prev = jnp.where(i > 0, prev_ref[...], 0.0)
    for j in range(nslab):
        ext_ref[j, 0:HALO_BLOCK, :] = prev[:, j * LANES:(j + 1) * LANES]
        ext_ref[j, HALO_BLOCK:, :] = cur_ref[:, j * LANES:(j + 1) * LANES]
    first = HALO_BLOCK - (CONV_KERNEL - 1) * SUBLANES
    rc = rows_per_chunk
    rcb = rc
    reps = rcb // SUBLANES

    def conv_block(n, carry):
        ci = n // nslab
        j = n % nslab
        r0 = pl.multiple_of(ci * rcb, rcb)
        accs = [cb_ref[j]] * reps
        for k in range(CONV_KERNEL):
            w = cw_ref[k * nslab + j]
            accs = [a + w * ext_ref[j, pl.ds(r0 + first + SUBLANES * (k + r), SUBLANES), :]
                    for r, a in enumerate(accs)]
        conv_ref[j, pl.ds(r0, rcb), :] = jnp.concatenate(accs, axis=0)
        return carry

    lax.fori_loop(0, (tm // rcb) * nslab, conv_block, 0)

    def norm_chunk(ci, carry):
        r0 = pl.multiple_of(ci * rc, rc)
        acc = jnp.concatenate([conv_ref[j, pl.ds(r0, rc), :] for j in range(nslab)], axis=-1)
        mu = jnp.mean(acc, axis=-1, keepdims=True)
        xc = acc - mu
        y = xc * lax.rsqrt(jnp.mean(xc * xc, axis=-1, keepdims=True) + LN_EPS)
        y = y * lg_ref[...] + lb_ref[...]
        act_ref[pl.ds(r0, rc), :] = jax.nn.silu(y).astype(BF16)
        return carry

    lax.fori_loop(0, tm // rc, norm_chunk, 0, unroll=4)
    out_ref[...] = _dot(act_ref[...], wpw_ref[...]).astype(BF16)


def _conv_branch(u, conv_w, conv_b, ln_g, ln_b, w_pw, layer, tm):
    t, c = u.shape
    d = w_pw.shape[-1]
    ratio = tm // HALO_BLOCK
    return pl.pallas_call(
        functools.partial(_conv_kernel, rows_per_chunk=64),
        grid=(t // tm,),
        in_specs=[pl.BlockSpec((tm, c), lambda i: (i, 0)),
                  pl.BlockSpec((HALO_BLOCK, c), lambda i: (jnp.maximum(i * ratio - 1, 0), 0)),
                  pl.BlockSpec((None,) + conv_w.shape[1:], lambda i: (layer, 0, 0, 0)),
                  pl.BlockSpec((None,) + conv_b.shape[1:], lambda i: (layer, 0, 0, 0)),
                  pl.BlockSpec((None, 1, c), lambda i: (layer, 0, 0)),
                  pl.BlockSpec((None, 1, c), lambda i: (layer, 0, 0)),
                  pl.BlockSpec((None, c, d), lambda i: (layer, 0, 0))],
        out_specs=pl.BlockSpec((tm, d), lambda i: (i, 0)),
        out_shape=jax.ShapeDtypeStruct((t, d), BF16),
        scratch_shapes=[pltpu.VMEM((c // LANES, tm + HALO_BLOCK, LANES), F32),
                        pltpu.VMEM((c // LANES, tm, LANES), F32), pltpu.VMEM((tm, c), BF16)],
        compiler_params=_params("parallel"),
        name="conv_branch",
    )(u, u, conv_w, conv_b, ln_g, ln_b, w_pw)


def _ssm_kernel(u_ref, bm_ref, cm_ref, a_ref, d_ref, out_ref, hs_ref, st_ref, *, nsplit):
    step_idx = pl.program_id(0)
    rows, width = u_ref.shape
    hw = width // 2
    hs_cols = hs_ref.shape[1] // 2
    hstates = hs_cols // 2
    sub = rows // nsplit

    @pl.when(step_idx == 0)
    def _():
        st_ref[...] = jnp.zeros_like(st_ref)

    cols = [(hf * hs_cols + j * LANES, hf * hs_cols + hstates + j * LANES, hf * hstates + j * LANES)
            for hf in range(2) for j in range(hstates // LANES)]

    def input_matmul(p):
        part = slice(p * sub, (p + 1) * sub)
        for hf in range(2):
            ub = u_ref[part, hf * hw:(hf + 1) * hw].astype(BF16)
            hs_ref[part, hf * hs_cols:(hf + 1) * hs_cols] = _dot(ub, bm_ref[hf])

    def scan(p):
        for cr, ci, ac in cols:
            ar = a_ref[0:SUBLANES, ac:ac + LANES]
            ai = a_ref[SUBLANES:2 * SUBLANES, ac:ac + LANES]
            hr = st_ref[:, cr:cr + LANES]
            hi = st_ref[:, ci:ci + LANES]
            for t in range(sub // SUBLANES):
                step = slice(p * sub + t * SUBLANES, p * sub + (t + 1) * SUBLANES)
                bur = hs_ref[step, cr:cr + LANES]
                bui = hs_ref[step, ci:ci + LANES]
                hr, hi = ar * hr - ai * hi + bur, ar * hi + ai * hr + bui
                hs_ref[step, cr:cr + LANES] = hr
                hs_ref[step, ci:ci + LANES] = hi
            st_ref[:, cr:cr + LANES] = hr
            st_ref[:, ci:ci + LANES] = hi

    def output_matmul(p):
        part = slice(p * sub, (p + 1) * sub)
        for hf in range(2):
            half = slice(hf * hw, (hf + 1) * hw)
            hb = hs_ref[part, hf * hs_cols:(hf + 1) * hs_cols].astype(BF16)
            y = _dot(hb, cm_ref[hf]) + d_ref[:, half] * u_ref[part, half]
            out_ref[part, half] = jax.nn.gelu(y).astype(BF16)

    input_matmul(0)
    for p in range(nsplit):
        if p + 1 < nsplit:
            input_matmul(p + 1)
        scan(p)
        output_matmul(p)


def _ssm_branch(u, bm, cm, a_rows, d_row, rows):
    t, width = u.shape
    ncols = 2 * bm.shape[-1]
    return pl.pallas_call(
        functools.partial(_ssm_kernel, nsplit=2),
        grid=(t // rows,),
        in_specs=[pl.BlockSpec((rows, width), lambda i: (i, 0)),
                  pl.BlockSpec(bm.shape, lambda i: (0, 0, 0)),
                  pl.BlockSpec(cm.shape, lambda i: (0, 0, 0)),
                  pl.BlockSpec(a_rows.shape, lambda i: (0, 0)),
                  pl.BlockSpec(d_row.shape, lambda i: (0, 0))],
        out_specs=pl.BlockSpec((rows, width), lambda i: (i, 0)),
        out_shape=jax.ShapeDtypeStruct((t, width), BF16),
        scratch_shapes=[pltpu.VMEM((rows, ncols), F32), pltpu.VMEM((SUBLANES, ncols), F32)],
        compiler_params=_params("arbitrary"),
        name="ssm_branch",
    )(u, bm, cm, a_rows, d_row)


def _ssm_tables(lam_re, lam_im, log_dt, b_re, b_im, c_re, c_im, d_skip):
    g, p = lam_re.shape
    gh = g // 2
    dt = jnp.exp(log_dt)[:, None]
    mag = jnp.exp(lam_re * dt)
    ar = mag * jnp.cos(lam_im * dt)
    ai = mag * jnp.sin(lam_im * dt)
    den = lam_re * lam_re + lam_im * lam_im
    xr = ar - 1.0
    kr = (xr * lam_re + ai * lam_im) / den
    ki = (ai * lam_re - xr * lam_im) / den
    bbar_r = kr[..., None] * b_re - ki[..., None] * b_im
    bbar_i = kr[..., None] * b_im + ki[..., None] * b_re
    eye = jnp.eye(gh, dtype=F32)

    def bd_in(m):
        m = m.reshape(2, gh, p, SSM_GROUP)
        return jnp.einsum("aqph,qr->aqhrp", m, eye).reshape(2, gh * SSM_GROUP, gh * p)

    def bd_out(m):
        m = m.reshape(2, gh, SSM_GROUP, p)
        return jnp.einsum("aqhp,qr->aqprh", m, eye).reshape(2, gh * p, gh * SSM_GROUP)

    bm = jnp.concatenate([bd_in(bbar_r), bd_in(bbar_i)], axis=-1).astype(BF16)
    cm = jnp.concatenate([bd_out(c_re), -bd_out(c_im)], axis=1).astype(BF16)
    a_rows = jnp.concatenate([jnp.broadcast_to(ar.reshape(1, g * p), (SUBLANES, g * p)),
                              jnp.broadcast_to(ai.reshape(1, g * p), (SUBLANES, g * p))], axis=0)
    return bm, cm, a_rows, d_skip.reshape(1, g * SSM_GROUP)


def _slabs_store(val, slab_ref, rows=slice(None)):
    for j in range(slab_ref.shape[0]):
        slab_ref[j, rows, :] = val[:, j * LANES:(j + 1) * LANES]


def _slabs_to_batches(slab_ref, out_ref):
    nb, n, d = out_ref.shape
    for b in range(nb):
        rows = [slab_ref[j, pl.ds(b, n, stride=nb), :] for j in range(d // LANES)]
        out_ref[b] = jnp.concatenate(rows, axis=-1).astype(out_ref.dtype)


def _batches_from_rows(val, slab_ref, out_ref):
    _slabs_store(val, slab_ref)
    _slabs_to_batches(slab_ref, out_ref)


def _batches_to_slabs(in_ref, slab_ref):
    nb, n, d = in_ref.shape
    for b in range(nb):
        xb = in_ref[b].astype(F32)
        for j in range(d // LANES):
            slab_ref[j, pl.ds(b, n, stride=nb), :] = xb[:, j * LANES:(j + 1) * LANES]


def _slab_rows(slab_ref, rows=slice(None)):
    return jnp.concatenate([slab_ref[j, rows, :] for j in range(slab_ref.shape[0])], axis=-1)


def _rows_from_batches(in_ref, slab_ref):
    _batches_to_slabs(in_ref, slab_ref)
    return _slab_rows(slab_ref)


def _merge_kernel(ys_ref, wglu_ref, gt_ref, ya_ref, h_ref, wout_ref, gx_ref, wq_ref, hout_ref, q_ref,
                  slab_ref, *, nsplit):
    tm, d = h_ref.shape
    for r in _row_parts(tm, nsplit):
        z = _dot(ys_ref[r, :], wglu_ref[...])
        yb = z[:, :d] * jax.nn.sigmoid(z[:, d:])
        m = gt_ref[r, :d].astype(F32) * ya_ref[r, :].astype(F32) + gt_ref[r, d:].astype(F32) * yb
        h2 = h_ref[r, :] + _dot(m.astype(BF16), wout_ref[...])
        hout_ref[r, :] = h2
        _slabs_store(_dot(_rms(h2, gx_ref[...]).astype(BF16), wq_ref[...]), slab_ref, r)
    _slabs_to_batches(slab_ref, q_ref)


def _merge(ys, w_glu, gates, ya, h, w_out, gx, w_q, layer, nb, tm):
    t, d = h.shape
    sw = ys.shape[1]
    n = tm // nb
    return pl.pallas_call(
        functools.partial(_merge_kernel, nsplit=1),
        grid=(t // tm,),
        in_specs=[pl.BlockSpec((tm, sw), lambda i: (i, 0)),
                  pl.BlockSpec((None, sw, 2 * d), lambda i: (layer, 0, 0)),
                  pl.BlockSpec((tm, 2 * d), lambda i: (i, 0)),
                  pl.BlockSpec((tm, d), lambda i: (i, 0)),
                  pl.BlockSpec((tm, d), lambda i: (i, 0)),
                  pl.BlockSpec((None, d, d), lambda i: (layer, 0, 0)),
                  pl.BlockSpec((None, 1, d), lambda i: (layer, 0, 0)),
                  pl.BlockSpec((None, d, d), lambda i: (layer, 0, 0))],
        out_specs=[pl.BlockSpec((tm, d), lambda i: (i, 0)),
                   pl.BlockSpec((nb, n, d), lambda i: (0, i, 0))],
        out_shape=[jax.ShapeDtypeStruct((t, d), F32), jax.ShapeDtypeStruct((nb, t // nb, d), BF16)],
        scratch_shapes=[pltpu.VMEM((d // LANES, tm, LANES), F32)],
        compiler_params=_params("parallel"),
        name="merge",
    )(ys, w_glu, gates, ya, h, w_out, gx, w_q)


def _kv_kernel(mem_ref, g_ref, w_ref, out_ref):
    xn = _rms(mem_ref[...], g_ref[...]).astype(BF16)
    out_ref[...] = _dot(xn, w_ref[...]).astype(BF16)


def _kv_proj(mem2d, g, w_kv):
    nl, d, d2 = w_kv.shape
    bm = mem2d.shape[0]
    return pl.pallas_call(
        _kv_kernel,
        grid=(nl, d2 // d),
        in_specs=[pl.BlockSpec((bm, d), lambda l, j: (0, 0)),
                  pl.BlockSpec((1, d), lambda l, j: (0, 0)),
                  pl.BlockSpec((None, d, d), lambda l, j: (l, 0, j))],
        out_specs=pl.BlockSpec((None, bm, d), lambda l, j: (l, 0, j)),
        out_shape=jax.ShapeDtypeStruct((nl, bm, d2), BF16),
        compiler_params=_params("parallel", "parallel"),
        name="kv_proj",
    )(mem2d, g, w_kv)


def _attn_kernel(q_ref, kv_ref, out_ref):
    d = q_ref.shape[1]
    hd = d // XATTN_HEADS
    scale = 1.0 / math.sqrt(hd)
    for n in range(XATTN_HEADS):
        qh = q_ref[:, n * hd:(n + 1) * hd]
        kh = kv_ref[:, n * hd:(n + 1) * hd]
        vh = kv_ref[:, d + n * hd:d + (n + 1) * hd]
        s = lax.dot_general(qh, kh, (((1,), (1,)), ((), ())), preferred_element_type=F32) * scale
        e = jnp.exp(s - jnp.max(s, axis=-1, keepdims=True))
        p = e / jnp.sum(e, axis=-1, keepdims=True)
        out_ref[:, n * hd:(n + 1) * hd] = _dot(p.astype(BF16), vh).astype(out_ref.dtype)


def _attention(q, kv, layer, ts):
    nb, s, d = q.shape
    m = kv.shape[1] // nb
    return pl.pallas_call(
        _attn_kernel,
        grid=(nb, s // ts),
        in_specs=[pl.BlockSpec((None, ts, d), lambda b, i: (b, i, 0)),
                  pl.BlockSpec((None, m, 2 * d), lambda b, i: (layer, b, 0))],
        out_specs=pl.BlockSpec((None, ts, d), lambda b, i: (b, i, 0)),
        out_shape=jax.ShapeDtypeStruct((nb, s, d), BF16),
        compiler_params=_params("parallel", "parallel"),
        name="xattn",
    )(q, kv)


def _attn_residual(h_ref, wo_ref, slab_ref, rows=slice(None)):
    return h_ref[rows, :] + _dot(_slab_rows(slab_ref, rows).astype(BF16), wo_ref[...])


def _swiglu_step(xn, wg_ref, wu_ref, wd_ref):
    g = _dot(xn, wg_ref[...].astype(BF16))
    u = _dot(xn, wu_ref[...].astype(BF16))
    return _dot((jax.nn.silu(g) * u).astype(BF16), wd_ref[...].astype(BF16))


def _ffn_kernel(o_ref, h_ref, wo_ref, g_ref, wgu_ref, wd_ref, out_ref, slab_ref, *, tf):
    ff = wd_ref.shape[0]
    _batches_to_slabs(o_ref, slab_ref)
    h2 = _attn_residual(h_ref, wo_ref, slab_ref)
    xn = _rms(h2, g_ref[...]).astype(BF16)
    out_ref[...] = h2
    for c in range(ff // tf):
        g = _dot(xn, wgu_ref[:, c * tf:(c + 1) * tf])
        u = _dot(xn, wgu_ref[:, ff + c * tf:ff + (c + 1) * tf])
        out_ref[...] += _dot((jax.nn.silu(g) * u).astype(BF16), wd_ref[c * tf:(c + 1) * tf, :])


def _ff_tile(ff):
    for tf in (512, 256, 128):
        if ff % tf == 0:
            return tf
    raise ValueError(f"hidden size {ff} is not a multiple of {LANES}")


def _dense_ffn(o, h, w_o, g, w_gu, w_d, layer, j, tm):
    t, d = h.shape
    nb = o.shape[0]
    n = tm // nb
    ff = w_d.shape[1]
    once = dict(pipeline_mode=pl.Buffered(1))
    return pl.pallas_call(
        functools.partial(_ffn_kernel, tf=_ff_tile(ff)),
        grid=(t // tm,),
        in_specs=[pl.BlockSpec((nb, n, d), lambda i: (0, i, 0)),
                  pl.BlockSpec((tm, d), lambda i: (i, 0)),
                  pl.BlockSpec((None, d, d), lambda i: (layer, 0, 0), **once),
                  pl.BlockSpec((None, 1, d), lambda i: (layer, 0, 0)),
                  pl.BlockSpec((None, d, 2 * ff), lambda i: (j, 0, 0), **once),
                  pl.BlockSpec((None, ff, d), lambda i: (j, 0, 0), **once)],
        out_specs=pl.BlockSpec((tm, d), lambda i: (i, 0)),
        out_shape=jax.ShapeDtypeStruct((t, d), F32),
        scratch_shapes=[pltpu.VMEM((d // LANES, tm, LANES), F32)],
        compiler_params=_params("parallel"),
        name="dense_ffn",
    )(o, h, w_o, g, w_gu, w_d)


def _router_kernel(o_ref, h_ref, wo_ref, g_ref, rw_ref, rb_ref, h2_ref, hn_ref, idx_ref, wt_ref, slab_ref,
                   *, nsplit):
    _batches_to_slabs(o_ref, slab_ref)
    for r in _row_parts(h_ref.shape[0], nsplit):
        _route_rows(r, h_ref, wo_ref, g_ref, rw_ref, rb_ref, h2_ref, hn_ref, idx_ref, wt_ref, slab_ref)


def _route_rows(r, h_ref, wo_ref, g_ref, rw_ref, rb_ref, h2_ref, hn_ref, idx_ref, wt_ref, slab_ref):
    h2 = _attn_residual(h_ref, wo_ref, slab_ref, r)
    h2_ref[r, :] = h2
    xn = _rms(h2, g_ref[...])
    xh = xn.astype(BF16)
    hn_ref[r, :] = xn
    xl = (xn - xh.astype(F32)).astype(BF16)
    ph = _dot(xh, rw_ref[...])
    logits = ph[:, :LANES] + ph[:, LANES:] + _dot(xl, rw_ref[:, :LANES]) + rb_ref[...]
    lane = lax.broadcasted_iota(jnp.int32, logits.shape, 1)
    lanef = lane.astype(F32)
    neg = jnp.float32(-jnp.inf)
    big = jnp.float32(LANES)
    l1 = jnp.where(lane < N_EXPERTS, logits, neg)
    m1 = jnp.max(l1, axis=-1, keepdims=True)
    i1 = jnp.min(jnp.where(l1 == m1, lanef, big), axis=-1, keepdims=True)
    l2 = jnp.where(lanef == i1, neg, l1)
    m2 = jnp.max(l2, axis=-1, keepdims=True)
    i2 = jnp.min(jnp.where(l2 == m2, lanef, big), axis=-1, keepdims=True)
    e = jnp.exp(m2 - m1)
    w1 = 1.0 / (1.0 + e)
    w2 = e / (1.0 + e)
    top = jnp.float32(N_EXPERTS - 1)
    i1, i2 = jnp.minimum(i1, top), jnp.minimum(i2, top)
    idx_ref[r, :] = jnp.where(lane == 0, i1, jnp.where(lane == 1, i2, 0.0)).astype(jnp.int32)
    wt_ref[r, :] = jnp.where(lane == 0, w1, jnp.where(lane == 1, w2, 0.0))


def _router(o, h, w_o, g, rw, rb, layer, tm):
    t, d = h.shape
    nb = o.shape[0]
    n = tm // nb
    return pl.pallas_call(
        functools.partial(_router_kernel, nsplit=1),
        grid=(t // tm,),
        in_specs=[pl.BlockSpec((nb, n, d), lambda i: (0, i, 0)),
                  pl.BlockSpec((tm, d), lambda i: (i, 0)),
                  pl.BlockSpec((None, d, d), lambda i: (layer, 0, 0)),
                  pl.BlockSpec((None, 1, d), lambda i: (layer, 0, 0)),
                  pl.BlockSpec((d, 2 * LANES), lambda i: (0, 0)),
                  pl.BlockSpec((1, LANES), lambda i: (0, 0))],
        out_specs=[pl.BlockSpec((tm, d), lambda i: (i, 0)),
                   pl.BlockSpec((tm, d), lambda i: (i, 0)),
                   pl.BlockSpec((tm, LANES), lambda i: (i, 0)),
                   pl.BlockSpec((tm, LANES), lambda i: (i, 0))],
        out_shape=[jax.ShapeDtypeStruct((t, d), F32),
                   jax.ShapeDtypeStruct((t, d), F32),
                   jax.ShapeDtypeStruct((t, LANES), jnp.int32),
                   jax.ShapeDtypeStruct((t, LANES), F32)],
        scratch_shapes=[pltpu.VMEM((d // LANES, tm, LANES), F32)],
        compiler_params=_params("parallel"),
        name="router",
    )(o, h, w_o, g, rw, rb)


def _moe_kernel(te_ref, tv_ref, x_ref, wg_ref, wu_ref, wd_ref, *rest, tile0):
    out_ref, xb_ref = rest[-2:]
    i = tile0 + pl.program_id(0)
    f = pl.program_id(1)
    valid = tv_ref[i] > 0

    @pl.when(f == 0)
    def _():
        xb_ref[...] = x_ref[...].astype(BF16)
        out_ref[...] = jnp.zeros_like(out_ref)

    @pl.when(valid)
    def _():
        out_ref[...] += _swiglu_step(xb_ref[...], wg_ref, wu_ref, wd_ref)


def _moe_experts(xg, w_gu, w_d, tile_expert, tile_valid, j, tm, tile0, total_rows, prev):
    r, d = xg.shape
    ff = w_d.shape[2]
    tf = _ff_tile(ff)
    nf = ff // tf
    in_specs = [pl.BlockSpec((tm, d), lambda i, f, te, tv: (i, 0)),
                pl.BlockSpec((None, None, d, tf), lambda i, f, te, tv: (j, te[tile0 + i], 0, f)),
                pl.BlockSpec((None, None, d, tf), lambda i, f, te, tv: (j, te[tile0 + i], 0, nf + f)),
                pl.BlockSpec((None, None, tf, d), lambda i, f, te, tv: (j, te[tile0 + i], f, 0))]
    args = [tile_expert, tile_valid, xg, w_gu, w_gu, w_d]
    aliases = {}
    if prev is not None:
        in_specs.append(pl.BlockSpec(memory_space=pl.ANY))
        aliases = {len(args): 0}
        args.append(prev)
    grid_spec = pltpu.PrefetchScalarGridSpec(
        num_scalar_prefetch=2,
        grid=(r // tm, nf),
        in_specs=in_specs,
        out_specs=pl.BlockSpec((tm, d), lambda i, f, te, tv: (tile0 + i, 0)),
        scratch_shapes=[pltpu.VMEM((tm, d), BF16)])
    return pl.pallas_call(
        functools.partial(_moe_kernel, tile0=tile0),
        grid_spec=grid_spec,
        out_shape=jax.ShapeDtypeStruct((total_rows, d), F32),
        input_output_aliases=aliases,
        compiler_params=_params("parallel", "arbitrary"),
        name="moe_experts",
    )(*args)


def _dispatch_tables(idx, tm):
    t = idx.shape[0]
    npairs = t * TOP_K
    ntiles = npairs // tm + N_EXPERTS
    e_flat = idx.reshape(npairs)
    onehot = (e_flat[:, None] == jnp.arange(N_EXPERTS, dtype=jnp.int32)[None, :]).astype(jnp.int32)
    csum = jnp.cumsum(onehot, axis=0)
    counts = csum[-1]
    rank = jnp.take_along_axis(csum, e_flat[:, None], axis=1)[:, 0] - 1
    padded = ((counts + tm - 1) // tm) * tm
    pend = jnp.cumsum(padded)
    pstart = pend - padded
    cstart = jnp.cumsum(counts) - counts
    pos = pstart[e_flat] + rank
    order = jnp.argsort(e_flat * npairs + jnp.arange(npairs, dtype=jnp.int32))
    tile_start = jnp.arange(ntiles, dtype=jnp.int32) * tm
    tile_expert = jnp.minimum(jnp.searchsorted(pend, tile_start, side="right"),
                              N_EXPERTS - 1).astype(jnp.int32)
    tile_valid = (tile_start < pend[-1]).astype(jnp.int32)
    slot = jnp.arange(ntiles * tm, dtype=jnp.int32)
    slot_e = jnp.repeat(tile_expert, tm)
    within = slot - pstart[slot_e]
    src = jnp.clip(cstart[slot_e] + within, 0, npairs - 1)
    row_token = jnp.where(within < counts[slot_e], order[src] // TOP_K, slot % t).astype(jnp.int32)
    return row_token, pos.reshape(t, TOP_K), tile_expert, tile_valid


def _final_kernel(*refs, kind):
    *stream, g_ref, out_ref, slab_ref = refs
    x = stream[0][...] if kind == "rows" else _moe_combine(*stream)
    _batches_from_rows(_rms(x, g_ref[...]), slab_ref, out_ref)


def _final_norm(kind, stream, g, nb, tm):
    t, d = stream[0].shape
    return pl.pallas_call(
        functools.partial(_final_kernel, kind=kind),
        grid=(t // tm,),
        in_specs=_stream_specs(kind, stream, tm) + [pl.BlockSpec((1, d), lambda i: (0, 0))],
        out_specs=pl.BlockSpec((nb, tm // nb, d), lambda i: (0, i, 0)),
        out_shape=jax.ShapeDtypeStruct((nb, t // nb, d), F32),
        scratch_shapes=[pltpu.VMEM((d // LANES, tm, LANES), F32)],
        compiler_params=_params("parallel"),
        name="final_norm",
    )(*stream, g)


def kernel(x, mem, mem_norm_g, norm_mix_g, w_in, conv_w, conv_b, conv_ln_g, conv_ln_b, w_conv_pw,
           ssm_lambda_re, ssm_lambda_im, ssm_log_dt, ssm_b_re, ssm_b_im, ssm_c_re, ssm_c_im, ssm_d,
           w_ssm_glu, w_out, norm_xattn_g, w_q, w_kv, w_o, norm_ffn_g, ffn_w_gate_up, ffn_w_down,
           router_w, router_b, moe_w_gate_up, moe_w_down, final_norm_g):
    nb, s, d = x.shape
    assert nb == SUBLANES, "the row layout puts the batch on the sublane axis"
    depth = w_in.shape[0]
    cw = conv_w.shape[-1]
    sw = ssm_d.shape[1] * ssm_d.shape[2]
    t = nb * s
    tm = min(512, t)
    tm_ff = min(1024, t)
    ts = min(512, s)

    w_in_b = w_in.astype(BF16)
    w_pw_b = w_conv_pw.astype(BF16)
    w_glu_b = w_ssm_glu.astype(BF16)
    w_out_b = w_out.astype(BF16)
    w_q_b = w_q.astype(BF16)
    w_kv_b = w_kv.astype(BF16)
    w_o_b = w_o.astype(BF16)
    ffn_gu_b = ffn_w_gate_up.astype(BF16)
    ffn_d_b = ffn_w_down.astype(BF16)
    nslab = cw // LANES
    conv_w_rep = jnp.broadcast_to(conv_w.reshape(depth, CONV_KERNEL * nslab, 1, LANES),
                                  (depth, CONV_KERNEL * nslab, SUBLANES, LANES))
    conv_b_rep = jnp.broadcast_to(conv_b.reshape(depth, nslab, 1, LANES), (depth, nslab, SUBLANES, LANES))
    row3 = lambda a: a.reshape(a.shape[0], 1, a.shape[1])
    g_mix, g_x, g_ffn = row3(norm_mix_g), row3(norm_xattn_g), row3(norm_ffn_g)
    lg3, lb3 = row3(conv_ln_g), row3(conv_ln_b)
    rw_pad = jnp.pad(router_w, ((0, 0), (0, 0), (0, LANES - N_EXPERTS)))
    rw_hi = rw_pad.astype(BF16)
    rw_pad = jnp.concatenate([rw_hi, (rw_pad - rw_hi.astype(F32)).astype(BF16)], axis=-1)
    rb_pad = jnp.pad(router_b, ((0, 0), (0, LANES - N_EXPERTS)))

    kv = _kv_proj(mem.reshape(nb * mem.shape[1], d), mem_norm_g.reshape(1, d), w_kv_b)
    kind, stream = "batches", (x,)

    for i in range(depth):
        tm_in = tm_ff if kind == "rows" else tm
        u_conv, u_ssm, gates, *rows = _in_proj(kind, stream, g_mix, w_in_b, i, cw, sw, tm_in)
        h = rows[0] if rows else stream[0]
        y_a = _conv_branch(u_conv, conv_w_rep, conv_b_rep, lg3, lb3, w_pw_b, i, tm)
        tables = _ssm_tables(ssm_lambda_re[i], ssm_lambda_im[i], ssm_log_dt[i], ssm_b_re[i], ssm_b_im[i],
                             ssm_c_re[i], ssm_c_im[i], ssm_d[i])
        y_s = _ssm_branch(u_ssm, *tables, rows=tm)
        h, q = _merge(y_s, w_glu_b, gates, y_a, h, w_out_b, g_x, w_q_b, i, nb, tm)
        o = _attention(q, kv, i, ts)
        j = i // 2
        if i % 2 == 0:
            h = _dense_ffn(o, h, w_o_b, g_ffn, ffn_gu_b, ffn_d_b, i, j, tm)
            kind, stream = "rows", (h,)
        else:
            h, hn, idx, wts = _router(o, h, w_o_b, g_ffn, rw_pad[j], rb_pad[j:j + 1], i, tm)
            row_token, pos, tile_expert, tile_valid = _dispatch_tables(idx[:, :TOP_K], tm_ff)
            rows_of = lambda a, ix: a.at[ix].get(mode="promise_in_bounds")
            ntiles = tile_expert.shape[0]
            bounds = sorted({0, ntiles // 9, ntiles // 3, (2 * ntiles) // 3, ntiles})
            yg = None
            for lo, hi in zip(bounds[:-1], bounds[1:]):
                xg = rows_of(hn, row_token[lo * tm_ff:hi * tm_ff])
                yg = _moe_experts(xg, moe_w_gate_up, moe_w_down, tile_expert, tile_valid, j, tm_ff,
                                  lo, ntiles * tm_ff, yg)
            kind, stream = "moe", (h, rows_of(yg, pos[:, 0]), rows_of(yg, pos[:, 1]), wts)

    return _final_norm(kind, stream, final_norm_g.reshape(1, d), nb, tm)
```

```python
import functools
import math

import jax
import jax.numpy as jnp
from jax import lax
from jax.experimental import pallas as pl
from jax.experimental.pallas import tpu as pltpu

F32 = jnp.float32
BF16 = jnp.bfloat16

RMS_EPS = 1e-6
LN_EPS = 1e-5
CONV_KERNEL = 31
SSM_GROUP = 16
SSM_STATE = 64
XATTN_HEADS = 4
N_EXPERTS = 8
TOP_K = 2

SUBLANES = 8
LANES = 128
HALO_BLOCK = 256
ROW_PART = 256
VMEM_LIMIT = 52 * 2 ** 20


def _params(*sem):
    return pltpu.CompilerParams(dimension_semantics=sem, vmem_limit_bytes=VMEM_LIMIT)


def _rms(x, g):
    return x * lax.rsqrt(jnp.mean(x * x, axis=-1, keepdims=True) + RMS_EPS) * g


def _dot(a, b):
    return jnp.dot(a, b, preferred_element_type=F32)


def _moe_combine(h_ref, y0_ref, y1_ref, wt_ref, rows=slice(None)):
    return h_ref[rows, :] + wt_ref[rows, 0:1] * y0_ref[rows, :] + wt_ref[rows, 1:2] * y1_ref[rows, :]


def _row_parts(tm, nsplit):
    sub = tm // nsplit
    return [slice(p * sub, (p + 1) * sub) for p in range(nsplit)]


def _in_kernel(*refs, kind, cw, sw, nsplit):
    if kind == "rows":
        h_ref, g_ref, w_ref, uc_ref, us_ref, gt_ref = refs
        x_of = lambda r: h_ref[r, :]
    elif kind == "batches":
        h_ref, g_ref, w_ref, uc_ref, us_ref, gt_ref, hrow_ref, slab_ref = refs
        hrow_ref[...] = _rows_from_batches(h_ref, slab_ref)
        x_of = lambda r: hrow_ref[r, :]
    else:
        h_ref, y0_ref, y1_ref, wt_ref, g_ref, w_ref, uc_ref, us_ref, gt_ref, hrow_ref = refs

        def x_of(r):
            x = _moe_combine(h_ref, y0_ref, y1_ref, wt_ref, r)
            hrow_ref[r, :] = x
            return x

    for r in _row_parts(uc_ref.shape[0], nsplit):
        xn = _rms(x_of(r), g_ref[...]).astype(BF16)
        a = _dot(xn, w_ref[:, 0:cw])
        gate = _dot(xn, w_ref[:, cw:2 * cw])
        uc_ref[r, :] = a * jax.nn.sigmoid(gate)
        us_ref[r, :] = _dot(xn, w_ref[:, 2 * cw:2 * cw + sw])
        gt_ref[r, :] = jax.nn.sigmoid(_dot(xn, w_ref[:, 2 * cw + sw:])).astype(BF16)


def _stream_specs(kind, stream, tm):
    d = stream[0].shape[-1]
    row = pl.BlockSpec((tm, d), lambda i: (i, 0))
    if kind == "rows":
        return [row]
    if kind == "batches":
        nb = stream[0].shape[0]
        return [pl.BlockSpec((nb, tm // nb, d), lambda i: (0, i, 0))]
    return [row, row, row, pl.BlockSpec((tm, LANES), lambda i: (i, 0))]


def _in_proj(kind, stream, g, w, layer, cw, sw, tm):
    d = stream[0].shape[-1]
    t = stream[0].size // d
    cols = w.shape[-1]
    ng = cols - 2 * cw - sw
    out_specs = [pl.BlockSpec((tm, cw), lambda i: (i, 0)),
                 pl.BlockSpec((tm, sw), lambda i: (i, 0)),
                 pl.BlockSpec((tm, ng), lambda i: (i, 0))]
    out_shape = [jax.ShapeDtypeStruct((t, cw), F32),
                 jax.ShapeDtypeStruct((t, sw), F32),
                 jax.ShapeDtypeStruct((t, ng), BF16)]
    scratch = []
    if kind != "rows":
        out_specs.append(pl.BlockSpec((tm, d), lambda i: (i, 0)))
        out_shape.append(jax.ShapeDtypeStruct((t, d), F32))
    if kind == "batches":
        scratch.append(pltpu.VMEM((d // LANES, tm, LANES), F32))
    return pl.pallas_call(
        functools.partial(_in_kernel, kind=kind, cw=cw, sw=sw, nsplit=tm // ROW_PART),
        grid=(t // tm,),
        in_specs=_stream_specs(kind, stream, tm) + [
            pl.BlockSpec((None, 1, d), lambda i: (layer, 0, 0)),
            pl.BlockSpec((None, d, cols), lambda i: (layer, 0, 0), pipeline_mode=pl.Buffered(1))],
        out_specs=out_specs,
        out_shape=out_shape,
        scratch_shapes=scratch,
        compiler_params=_params("parallel"),
        name="in_proj",
    )(*stream, g, w)


def _conv_kernel(cur_ref, prev_ref, cw_ref, cb_ref, lg_ref, lb_ref, wpw_ref, out_ref,
                 ext_ref, conv_ref, act_ref, *, rows_per_chunk):
    i = pl.program_id(0)
    tm, c = cur_ref.shape
    nslab = c // LANES
    prev = jnp.where(i > 0, prev_ref[...], 0.0)
    for j in range(nslab):
        ext_ref[j, 0:HALO_BLOCK, :] = prev[:, j * LANES:(j + 1) * LANES]
        ext_ref[j, HALO_BLOCK:, :] = cur_ref[:, j * LANES:(j + 1) * LANES]
    first = HALO_BLOCK - (CONV_KERNEL - 1) * SUBLANES
    rc = rows_per_chunk
    rcb = rc
    reps = rcb // SUBLANES

    def conv_block(n, carry):
        ci = n // nslab
        j = n % nslab
        r0 = pl.multiple_of(ci * rcb, rcb)
        accs = [cb_ref[j]] * reps
        for k in range(CONV_KERNEL):
            w = cw_ref[k * nslab + j]
            accs = [a + w * ext_ref[j, pl.ds(r0 + first + SUBLANES * (k + r), SUBLANES), :]
                    for r, a in enumerate(accs)]
        conv_ref[j, pl.ds(r0, rcb), :] = jnp.concatenate(accs, axis=0)
        return carry

    lax.fori_loop(0, (tm // rcb) * nslab, conv_block, 0)

    def norm_chunk(ci, carry):
        r0 = pl.multiple_of(ci * rc, rc)
        acc = jnp.concatenate([conv_ref[j, pl.ds(r0, rc), :] for j in range(nslab)], axis=-1)
        mu = jnp.mean(acc, axis=-1, keepdims=True)
        xc = acc - mu
        y = xc * lax.rsqrt(jnp.mean(xc * xc, axis=-1, keepdims=True) + LN_EPS)
        y = y * lg_ref[...] + lb_ref[...]
        act_ref[pl.ds(r0, rc), :] = jax.nn.silu(y).astype(BF16)
        return carry

    lax.fori_loop(0, tm // rc, norm_chunk, 0, unroll=4)
    out_ref[...] = _dot(act_ref[...], wpw_ref[...]).astype(BF16)


def _conv_branch(u, conv_w, conv_b, ln_g, ln_b, w_pw, layer, tm):
    t, c = u.shape
    d = w_pw.shape[-1]
    ratio = tm // HALO_BLOCK
    return pl.pallas_call(
        functools.partial(_conv_kernel, rows_per_chunk=64),
        grid=(t // tm,),
        in_specs=[pl.BlockSpec((tm, c), lambda i: (i, 0)),
                  pl.BlockSpec((HALO_BLOCK, c), lambda i: (jnp.maximum(i * ratio - 1, 0), 0)),
                  pl.BlockSpec((None,) + conv_w.shape[1:], lambda i: (layer, 0, 0, 0)),
                  pl.BlockSpec((None,) + conv_b.shape[1:], lambda i: (layer, 0, 0, 0)),
                  pl.BlockSpec((None, 1, c), lambda i: (layer, 0, 0)),
                  pl.BlockSpec((None, 1, c), lambda i: (layer, 0, 0)),
                  pl.BlockSpec((None, c, d), lambda i: (layer, 0, 0))],
        out_specs=pl.BlockSpec((tm, d), lambda i: (i, 0)),
        out_shape=jax.ShapeDtypeStruct((t, d), BF16),
        scratch_shapes=[pltpu.VMEM((c // LANES, tm + HALO_BLOCK, LANES), F32),
                        pltpu.VMEM((c // LANES, tm, LANES), F32), pltpu.VMEM((tm, c), BF16)],
        compiler_params=_params("parallel"),
        name="conv_branch",
    )(u, u, conv_w, conv_b, ln_g, ln_b, w_pw)


def _ssm_kernel(u_ref, bm_ref, cm_ref, a_ref, d_ref, out_ref, hs_ref, st_ref, *, nsplit):
    step_idx = pl.program_id(0)
    rows, width = u_ref.shape
    hw = width // 2
    hs_cols = hs_ref.shape[1] // 2
    hstates = hs_cols // 2
    sub = rows // nsplit

    @pl.when(step_idx == 0)
    def _():
        st_ref[...] = jnp.zeros_like(st_ref)

    cols = [(hf * hs_cols + j * LANES, hf * hs_cols + hstates + j * LANES, hf * hstates + j * LANES)
            for hf in range(2) for j in range(hstates // LANES)]

    def input_matmul(p):
        part = slice(p * sub, (p + 1) * sub)
        for hf in range(2):
            ub = u_ref[part, hf * hw:(hf + 1) * hw].astype(BF16)
            hs_ref[part, hf * hs_cols:(hf + 1) * hs_cols] = _dot(ub, bm_ref[hf])

    def scan(p):
        for cr, ci, ac in cols:
            ar = a_ref[0:SUBLANES, ac:ac + LANES]
            ai = a_ref[SUBLANES:2 * SUBLANES, ac:ac + LANES]
            hr = st_ref[:, cr:cr + LANES]
            hi = st_ref[:, ci:ci + LANES]
            for t in range(sub // SUBLANES):
                step = slice(p * sub + t * SUBLANES, p * sub + (t + 1) * SUBLANES)
                bur = hs_ref[step, cr:cr + LANES]
                bui = hs_ref[step, ci:ci + LANES]
                hr, hi = ar * hr - ai * hi + bur, ar * hi + ai * hr + bui
                hs_ref[step, cr:cr + LANES] = hr
                hs_ref[step, ci:ci + LANES] = hi
            st_ref[:, cr:cr + LANES] = hr
            st_ref[:, ci:ci + LANES] = hi

    def output_matmul(p):
        part = slice(p * sub, (p + 1) * sub)
        for hf in range(2):
            half = slice(hf * hw, (hf + 1) * hw)
            hb = hs_ref[part, hf * hs_cols:(hf + 1) * hs_cols].astype(BF16)
            y = _dot(hb, cm_ref[hf]) + d_ref[:, half] * u_ref[part, half]
            out_ref[part, half] = jax.nn.gelu(y).astype(BF16)

    input_matmul(0)
    for p in range(nsplit):
        if p + 1 < nsplit:
            input_matmul(p + 1)
        scan(p)
        output_matmul(p)


def _ssm_branch(u, bm, cm, a_rows, d_row, rows):
    t, width = u.shape
    ncols = 2 * bm.shape[-1]
    return pl.pallas_call(
        functools.partial(_ssm_kernel, nsplit=2),
        grid=(t // rows,),
        in_specs=[pl.BlockSpec((rows, width), lambda i: (i, 0)),
                  pl.BlockSpec(bm.shape, lambda i: (0, 0, 0)),
                  pl.BlockSpec(cm.shape, lambda i: (0, 0, 0)),
                  pl.BlockSpec(a_rows.shape, lambda i: (0, 0)),
                  pl.BlockSpec(d_row.shape, lambda i: (0, 0))],
        out_specs=pl.BlockSpec((rows, width), lambda i: (i, 0)),
        out_shape=jax.ShapeDtypeStruct((t, width), BF16),
        scratch_shapes=[pltpu.VMEM((rows, ncols), F32), pltpu.VMEM((SUBLANES, ncols), F32)],
        compiler_params=_params("arbitrary"),
        name="ssm_branch",
    )(u, bm, cm, a_rows, d_row)


def _ssm_tables(lam_re, lam_im, log_dt, b_re, b_im, c_re, c_im, d_skip):
    g, p = lam_re.shape
    gh = g // 2
    dt = jnp.exp(log_dt)[:, None]
    mag = jnp.exp(lam_re * dt)
    ar = mag * jnp.cos(lam_im * dt)
    ai = mag * jnp.sin(lam_im * dt)
    den = lam_re * lam_re + lam_im * lam_im
    xr = ar - 1.0
    kr = (xr * lam_re + ai * lam_im) / den
    ki = (ai * lam_re - xr * lam_im) / den
    bbar_r = kr[..., None] * b_re - ki[..., None] * b_im
    bbar_i = kr[..., None] * b_im + ki[..., None] * b_re
    eye = jnp.eye(gh, dtype=F32)

    def bd_in(m):
        m = m.reshape(2, gh, p, SSM_GROUP)
        return jnp.einsum("aqph,qr->aqhrp", m, eye).reshape(2, gh * SSM_GROUP, gh * p)

    def bd_out(m):
        m = m.reshape(2, gh, SSM_GROUP, p)
        return jnp.einsum("aqhp,qr->aqprh", m, eye).reshape(2, gh * p, gh * SSM_GROUP)

    bm = jnp.concatenate([bd_in(bbar_r), bd_in(bbar_i)], axis=-1).astype(BF16)
    cm = jnp.concatenate([bd_out(c_re), -bd_out(c_im)], axis=1).astype(BF16)
    a_rows = jnp.concatenate([jnp.broadcast_to(ar.reshape(1, g * p), (SUBLANES, g * p)),
                              jnp.broadcast_to(ai.reshape(1, g * p), (SUBLANES, g * p))], axis=0)
    return bm, cm, a_rows, d_skip.reshape(1, g * SSM_GROUP)


def _slabs_store(val, slab_ref, rows=slice(None)):
    for j in range(slab_ref.shape[0]):
        slab_ref[j, rows, :] = val[:, j * LANES:(j + 1) * LANES]


def _slabs_to_batches(slab_ref, out_ref):
    nb, n, d = out_ref.shape
    for b in range(nb):
        rows = [slab_ref[j, pl.ds(b, n, stride=nb), :] for j in range(d // LANES)]
        out_ref[b] = jnp.concatenate(rows, axis=-1).astype(out_ref.dtype)


def _batches_from_rows(val, slab_ref, out_ref):
    _slabs_store(val, slab_ref)
    _slabs_to_batches(slab_ref, out_ref)


def _batches_to_slabs(in_ref, slab_ref):
    nb, n, d = in_ref.shape
    for b in range(nb):
        xb = in_ref[b].astype(F32)
        for j in range(d // LANES):
            slab_ref[j, pl.ds(b, n, stride=nb), :] = xb[:, j * LANES:(j + 1) * LANES]


def _slab_rows(slab_ref, rows=slice(None)):
    return jnp.concatenate([slab_ref[j, rows, :] for j in range(slab_ref.shape[0])], axis=-1)


def _rows_from_batches(in_ref, slab_ref):
    _batches_to_slabs(in_ref, slab_ref)
    return _slab_rows(slab_ref)


def _merge_kernel(ys_ref, wglu_ref, gt_ref, ya_ref, h_ref, wout_ref, gx_ref, wq_ref, hout_ref, q_ref,
                  slab_ref, *, nsplit):
    tm, d = h_ref.shape
    for r in _row_parts(tm, nsplit):
        z = _dot(ys_ref[r, :], wglu_ref[...])
        yb = z[:, :d] * jax.nn.sigmoid(z[:, d:])
        m = gt_ref[r, :d].astype(F32) * ya_ref[r, :].astype(F32) + gt_ref[r, d:].astype(F32) * yb
        h2 = h_ref[r, :] + _dot(m.astype(BF16), wout_ref[...])
        hout_ref[r, :] = h2
        _slabs_store(_dot(_rms(h2, gx_ref[...]).astype(BF16), wq_ref[...]), slab_ref, r)
    _slabs_to_batches(slab_ref, q_ref)


def _merge(ys, w_glu, gates, ya, h, w_out, gx, w_q, layer, nb, tm):
    t, d = h.shape
    sw = ys.shape[1]
    n = tm // nb
    return pl.pallas_call(
        functools.partial(_merge_kernel, nsplit=1),
        grid=(t // tm,),
        in_specs=[pl.BlockSpec((tm, sw), lambda i: (i, 0)),
                  pl.BlockSpec((None, sw, 2 * d), lambda i: (layer, 0, 0)),
                  pl.BlockSpec((tm, 2 * d), lambda i: (i, 0)),
                  pl.BlockSpec((tm, d), lambda i: (i, 0)),
                  pl.BlockSpec((tm, d), lambda i: (i, 0)),
                  pl.BlockSpec((None, d, d), lambda i: (layer, 0, 0)),
                  pl.BlockSpec((None, 1, d), lambda i: (layer, 0, 0)),
                  pl.BlockSpec((None, d, d), lambda i: (layer, 0, 0))],
        out_specs=[pl.BlockSpec((tm, d), lambda i: (i, 0)),
                   pl.BlockSpec((nb, n, d), lambda i: (0, i, 0))],
        out_shape=[jax.ShapeDtypeStruct((t, d), F32), jax.ShapeDtypeStruct((nb, t // nb, d), BF16)],
        scratch_shapes=[pltpu.VMEM((d // LANES, tm, LANES), F32)],
        compiler_params=_params("parallel"),
        name="merge",
    )(ys, w_glu, gates, ya, h, w_out, gx, w_q)


def _kv_kernel(mem_ref, g_ref, w_ref, out_ref):
    xn = _rms(mem_ref[...], g_ref[...]).astype(BF16)
    out_ref[...] = _dot(xn, w_ref[...]).astype(BF16)


def _kv_proj(mem2d, g, w_kv):
    nl, d, d2 = w_kv.shape
    bm = mem2d.shape[0]
    return pl.pallas_call(
        _kv_kernel,
        grid=(nl, d2 // d),
        in_specs=[pl.BlockSpec((bm, d), lambda l, j: (0, 0)),
                  pl.BlockSpec((1, d), lambda l, j: (0, 0)),
                  pl.BlockSpec((None, d, d), lambda l, j: (l, 0, j))],
        out_specs=pl.BlockSpec((None, bm, d), lambda l, j: (l, 0, j)),
        out_shape=jax.ShapeDtypeStruct((nl, bm, d2), BF16),
        compiler_params=_params("parallel", "parallel"),
        name="kv_proj",
    )(mem2d, g, w_kv)


def _attn_kernel(q_ref, kv_ref, out_ref):
    d = q_ref.shape[1]
    hd = d // XATTN_HEADS
    scale = 1.0 / math.sqrt(hd)
    for n in range(XATTN_HEADS):
        qh = q_ref[:, n * hd:(n + 1) * hd]
        kh = kv_ref[:, n * hd:(n + 1) * hd]
        vh = kv_ref[:, d + n * hd:d + (n + 1) * hd]
        s = lax.dot_general(qh, kh, (((1,), (1,)), ((), ())), preferred_element_type=F32) * scale
        e = jnp.exp(s - jnp.max(s, axis=-1, keepdims=True))
        p = e / jnp.sum(e, axis=-1, keepdims=True)
        out_ref[:, n * hd:(n + 1) * hd] = _dot(p.astype(BF16), vh).astype(out_ref.dtype)


def _attention(q, kv, layer, ts):
    nb, s, d = q.shape
    m = kv.shape[1] // nb
    return pl.pallas_call(
        _attn_kernel,
        grid=(nb, s // ts),
        in_specs=[pl.BlockSpec((None, ts, d), lambda b, i: (b, i, 0)),
                  pl.BlockSpec((None, m, 2 * d), lambda b, i: (layer, b, 0))],
        out_specs=pl.BlockSpec((None, ts, d), lambda b, i: (b, i, 0)),
        out_shape=jax.ShapeDtypeStruct((nb, s, d), BF16),
        compiler_params=_params("parallel", "parallel"),
        name="xattn",
    )(q, kv)


def _attn_residual(h_ref, wo_ref, slab_ref, rows=slice(None)):
    return h_ref[rows, :] + _dot(_slab_rows(slab_ref, rows).astype(BF16), wo_ref[...])


def _swiglu_step(xn, wg_ref, wu_ref, wd_ref):
    g = _dot(xn, wg_ref[...].astype(BF16))
    u = _dot(xn, wu_ref[...].astype(BF16))
    return _dot((jax.nn.silu(g) * u).astype(BF16), wd_ref[...].astype(BF16))


def _ffn_kernel(o_ref, h_ref, wo_ref, g_ref, wgu_ref, wd_ref, out_ref, slab_ref, *, tf):
    ff = wd_ref.shape[0]
    _batches_to_slabs(o_ref, slab_ref)
    h2 = _attn_residual(h_ref, wo_ref, slab_ref)
    xn = _rms(h2, g_ref[...]).astype(BF16)
    out_ref[...] = h2
    for c in range(ff // tf):
        g = _dot(xn, wgu_ref[:, c * tf:(c + 1) * tf])
        u = _dot(xn, wgu_ref[:, ff + c * tf:ff + (c + 1) * tf])
        out_ref[...] += _dot((jax.nn.silu(g) * u).astype(BF16), wd_ref[c * tf:(c + 1) * tf, :])


def _ff_tile(ff):
    for tf in (512, 256, 128):
        if ff % tf == 0:
            return tf
    raise ValueError(f"hidden size {ff} is not a multiple of {LANES}")


def _dense_ffn(o, h, w_o, g, w_gu, w_d, layer, j, tm):
    t, d = h.shape
    nb = o.shape[0]
    n = tm // nb
    ff = w_d.shape[1]
    once = dict(pipeline_mode=pl.Buffered(1))
    return pl.pallas_call(
        functools.partial(_ffn_kernel, tf=_ff_tile(ff)),
        grid=(t // tm,),
        in_specs=[pl.BlockSpec((nb, n, d), lambda i: (0, i, 0)),
                  pl.BlockSpec((tm, d), lambda i: (i, 0)),
                  pl.BlockSpec((None, d, d), lambda i: (layer, 0, 0), **once),
                  pl.BlockSpec((None, 1, d), lambda i: (layer, 0, 0)),
                  pl.BlockSpec((None, d, 2 * ff), lambda i: (j, 0, 0), **once),
                  pl.BlockSpec((None, ff, d), lambda i: (j, 0, 0), **once)],
        out_specs=pl.BlockSpec((tm, d), lambda i: (i, 0)),
        out_shape=jax.ShapeDtypeStruct((t, d), F32),
        scratch_shapes=[pltpu.VMEM((d // LANES, tm, LANES), F32)],
        compiler_params=_params("parallel"),
        name="dense_ffn",
    )(o, h, w_o, g, w_gu, w_d)


def _router_kernel(o_ref, h_ref, wo_ref, g_ref, rw_ref, rb_ref, tri_ref, h2_ref, hn_ref, wt_ref, route_ref,
                   cnt_ref, slab_ref, base_ref):
    @pl.when(pl.program_id(0) == 0)
    def _():
        base_ref[...] = jnp.zeros_like(base_ref)

    _batches_to_slabs(o_ref, slab_ref)
    h2 = _attn_residual(h_ref, wo_ref, slab_ref)
    h2_ref[...] = h2
    xn = _rms(h2, g_ref[...])
    xh = xn.astype(BF16)
    hn_ref[...] = xn
    xl = (xn - xh.astype(F32)).astype(BF16)
    ph = _dot(xh, rw_ref[...])
    logits = ph[:, :LANES] + ph[:, LANES:] + _dot(xl, rw_ref[:, :LANES]) + rb_ref[...]
    lane = lax.broadcasted_iota(jnp.int32, logits.shape, 1)
    lanef = lane.astype(F32)
    neg = jnp.float32(-jnp.inf)
    big = jnp.float32(LANES)
    l1 = jnp.where(lane < N_EXPERTS, logits, neg)
    m1 = jnp.max(l1, axis=-1, keepdims=True)
    i1 = jnp.min(jnp.where(l1 == m1, lanef, big), axis=-1, keepdims=True)
    l2 = jnp.where(lanef == i1, neg, l1)
    m2 = jnp.max(l2, axis=-1, keepdims=True)
    i2 = jnp.min(jnp.where(l2 == m2, lanef, big), axis=-1, keepdims=True)
    e = jnp.exp(m2 - m1)
    w1 = 1.0 / (1.0 + e)
    w2 = e / (1.0 + e)
    top = jnp.float32(N_EXPERTS - 1)
    i1, i2 = jnp.minimum(i1, top), jnp.minimum(i2, top)
    wt_ref[...] = jnp.where(lane == 0, w1, jnp.where(lane == 1, w2, 0.0))
    hit1, hit2 = lanef == i1, lanef == i2
    seen = _dot(tri_ref[...], jnp.where(hit1 | hit2, 1.0, 0.0).astype(BF16)) + base_ref[...]
    r1 = jnp.sum(jnp.where(hit1, seen, 0.0), axis=-1, keepdims=True) - 1.0
    r2 = jnp.sum(jnp.where(hit2, seen, 0.0), axis=-1, keepdims=True) - 1.0
    base_ref[...] = seen[seen.shape[0] - 1:, :]
    cnt_ref[...] = base_ref[...]
    packed = jnp.where(lane == 0, i1, jnp.where(lane == 1, i2, jnp.where(lane == 2, r1,
                                                                         jnp.where(lane == 3, r2, 0.0))))
    route_ref[...] = packed.T[:SUBLANES, :].astype(jnp.int32)


def _router(o, h, w_o, g, rw, rb, tri, layer, tm):
    t, d = h.shape
    nb = o.shape[0]
    n = tm // nb
    return pl.pallas_call(
        _router_kernel,
        grid=(t // tm,),
        in_specs=[pl.BlockSpec((nb, n, d), lambda i: (0, i, 0)),
                  pl.BlockSpec((tm, d), lambda i: (i, 0)),
                  pl.BlockSpec((None, d, d), lambda i: (layer, 0, 0)),
                  pl.BlockSpec((None, 1, d), lambda i: (layer, 0, 0)),
                  pl.BlockSpec((d, 2 * LANES), lambda i: (0, 0)),
                  pl.BlockSpec((1, LANES), lambda i: (0, 0)),
                  pl.BlockSpec((tm, tm), lambda i: (0, 0))],
        out_specs=[pl.BlockSpec((tm, d), lambda i: (i, 0)),
                   pl.BlockSpec((tm, d), lambda i: (i, 0)),
                   pl.BlockSpec((tm, LANES), lambda i: (i, 0)),
                   pl.BlockSpec((SUBLANES, tm), lambda i: (0, i)),
                   pl.BlockSpec((1, LANES), lambda i: (0, 0))],
        out_shape=[jax.ShapeDtypeStruct((t, d), F32),
                   jax.ShapeDtypeStruct((t, d), F32),
                   jax.ShapeDtypeStruct((t, LANES), F32),
                   jax.ShapeDtypeStruct((SUBLANES, t), jnp.int32),
                   jax.ShapeDtypeStruct((1, LANES), F32)],
        scratch_shapes=[pltpu.VMEM((d // LANES, tm, LANES), F32), pltpu.VMEM((1, LANES), F32)],
        compiler_params=_params("arbitrary"),
        name="router",
    )(o, h, w_o, g, rw, rb, tri)


def _moe_kernel(te_ref, tv_ref, x_ref, wg_ref, wu_ref, wd_ref, *rest, tile0):
    out_ref, xb_ref = rest[-2:]
    i = tile0 + pl.program_id(0)
    f = pl.program_id(1)
    valid = tv_ref[i] > 0

    @pl.when(f == 0)
    def _():
        xb_ref[...] = x_ref[...].astype(BF16)
        out_ref[...] = jnp.zeros_like(out_ref)

    @pl.when(valid)
    def _():
        out_ref[...] += _swiglu_step(xb_ref[...], wg_ref, wu_ref, wd_ref)


def _moe_experts(xg, w_gu, w_d, tile_expert, tile_valid, j, tm, tile0, total_rows, prev):
    r, d = xg.shape
    ff = w_d.shape[2]
    tf = _ff_tile(ff)
    nf = ff // tf
    in_specs = [pl.BlockSpec((tm, d), lambda i, f, te, tv: (i, 0)),
                pl.BlockSpec((None, None, d, tf), lambda i, f, te, tv: (j, te[tile0 + i], 0, f)),
                pl.BlockSpec((None, None, d, tf), lambda i, f, te, tv: (j, te[tile0 + i], 0, nf + f)),
                pl.BlockSpec((None, None, tf, d), lambda i, f, te, tv: (j, te[tile0 + i], f, 0))]
    args = [tile_expert, tile_valid, xg, w_gu, w_gu, w_d]
    aliases = {}
    if prev is not None:
        in_specs.append(pl.BlockSpec(memory_space=pl.ANY))
        aliases = {len(args): 0}
        args.append(prev)
    grid_spec = pltpu.PrefetchScalarGridSpec(
        num_scalar_prefetch=2,
        grid=(r // tm, nf),
        in_specs=in_specs,
        out_specs=pl.BlockSpec((tm, d), lambda i, f, te, tv: (tile0 + i, 0)),
        scratch_shapes=[pltpu.VMEM((tm, d), BF16)])
    return pl.pallas_call(
        functools.partial(_moe_kernel, tile0=tile0),
        grid_spec=grid_spec,
        out_shape=jax.ShapeDtypeStruct((total_rows, d), F32),
        input_output_aliases=aliases,
        compiler_params=_params("parallel", "arbitrary"),
        name="moe_experts",
    )(*args)


def _dispatch_tables(route, cnt, tm):
    t = route.shape[1]
    npairs = t * TOP_K
    ntiles = npairs // tm + N_EXPERTS
    counts = cnt[0, :N_EXPERTS].astype(jnp.int32)
    padded = ((counts + tm - 1) // tm) * tm
    pend = jnp.cumsum(padded)
    pstart = pend - padded
    cstart = jnp.cumsum(counts) - counts
    experts, ranks = route[0:TOP_K], route[TOP_K:2 * TOP_K]
    pos = ranks
    for e in range(N_EXPERTS):
        pos = pos + jnp.where(experts == e, pstart[e], 0)
    order = jnp.argsort(pos.reshape(npairs))
    tile_start = jnp.arange(ntiles, dtype=jnp.int32) * tm
    tile_expert = jnp.minimum(jnp.searchsorted(pend, tile_start, side="right"),
                              N_EXPERTS - 1).astype(jnp.int32)
    tile_valid = (tile_start < pend[-1]).astype(jnp.int32)
    slot = jnp.arange(ntiles * tm, dtype=jnp.int32)
    slot_e = jnp.repeat(tile_expert, tm)
    within = slot - pstart[slot_e]
    src = jnp.clip(cstart[slot_e] + within, 0, npairs - 1)
    row_token = jnp.where(within < counts[slot_e], order[src] % t, slot % t).astype(jnp.int32)
    return row_token, pos, tile_expert, tile_valid


def _final_kernel(*refs, kind):
    *stream, g_ref, out_ref, slab_ref = refs
    x = stream[0][...] if kind == "rows" else _moe_combine(*stream)
    _batches_from_rows(_rms(x, g_ref[...]), slab_ref, out_ref)


def _final_norm(kind, stream, g, nb, tm):
    t, d = stream[0].shape
    return pl.pallas_call(
        functools.partial(_final_kernel, kind=kind),
        grid=(t // tm,),
        in_specs=_stream_specs(kind, stream, tm) + [pl.BlockSpec((1, d), lambda i: (0, 0))],
        out_specs=pl.BlockSpec((nb, tm // nb, d), lambda i: (0, i, 0)),
        out_shape=jax.ShapeDtypeStruct((nb, t // nb, d), F32),
        scratch_shapes=[pltpu.VMEM((d // LANES, tm, LANES), F32)],
        compiler_params=_params("parallel"),
        name="final_norm",
    )(*stream, g)


def kernel(x, mem, mem_norm_g, norm_mix_g, w_in, conv_w, conv_b, conv_ln_g, conv_ln_b, w_conv_pw,
           ssm_lambda_re, ssm_lambda_im, ssm_log_dt, ssm_b_re, ssm_b_im, ssm_c_re, ssm_c_im, ssm_d,
           w_ssm_glu, w_out, norm_xattn_g, w_q, w_kv, w_o, norm_ffn_g, ffn_w_gate_up, ffn_w_down,
           router_w, router_b, moe_w_gate_up, moe_w_down, final_norm_g):
    nb, s, d = x.shape
    assert nb == SUBLANES, "the row layout puts the batch on the sublane axis"
    depth = w_in.shape[0]
    cw = conv_w.shape[-1]
    sw = ssm_d.shape[1] * ssm_d.shape[2]
    t = nb * s
    tm = min(512, t)
    tm_ff = min(1024, t)
    ts = min(512, s)

    w_in_b = w_in.astype(BF16)
    w_pw_b = w_conv_pw.astype(BF16)
    w_glu_b = w_ssm_glu.astype(BF16)
    w_out_b = w_out.astype(BF16)
    w_q_b = w_q.astype(BF16)
    w_kv_b = w_kv.astype(BF16)
    w_o_b = w_o.astype(BF16)
    ffn_gu_b = ffn_w_gate_up.astype(BF16)
    ffn_d_b = ffn_w_down.astype(BF16)
    nslab = cw // LANES
    conv_w_rep = jnp.broadcast_to(conv_w.reshape(depth, CONV_KERNEL * nslab, 1, LANES),
                                  (depth, CONV_KERNEL * nslab, SUBLANES, LANES))
    conv_b_rep = jnp.broadcast_to(conv_b.reshape(depth, nslab, 1, LANES), (depth, nslab, SUBLANES, LANES))
    row3 = lambda a: a.reshape(a.shape[0], 1, a.shape[1])
    g_mix, g_x, g_ffn = row3(norm_mix_g), row3(norm_xattn_g), row3(norm_ffn_g)
    lg3, lb3 = row3(conv_ln_g), row3(conv_ln_b)
    rw_pad = jnp.pad(router_w, ((0, 0), (0, 0), (0, LANES - N_EXPERTS)))
    rw_hi = rw_pad.astype(BF16)
    rw_pad = jnp.concatenate([rw_hi, (rw_pad - rw_hi.astype(F32)).astype(BF16)], axis=-1)
    rb_pad = jnp.pad(router_b, ((0, 0), (0, LANES - N_EXPERTS)))
    tri = jnp.tril(jnp.ones((tm, tm), BF16))

    kv = _kv_proj(mem.reshape(nb * mem.shape[1], d), mem_norm_g.reshape(1, d), w_kv_b)
    kind, stream = "batches", (x,)

    for i in range(depth):
        tm_in = tm_ff if kind == "rows" else tm
        u_conv, u_ssm, gates, *rows = _in_proj(kind, stream, g_mix, w_in_b, i, cw, sw, tm_in)
        h = rows[0] if rows else stream[0]
        y_a = _conv_branch(u_conv, conv_w_rep, conv_b_rep, lg3, lb3, w_pw_b, i, tm)
        tables = _ssm_tables(ssm_lambda_re[i], ssm_lambda_im[i], ssm_log_dt[i], ssm_b_re[i], ssm_b_im[i],
                             ssm_c_re[i], ssm_c_im[i], ssm_d[i])
        y_s = _ssm_branch(u_ssm, *tables, rows=tm)
        h, q = _merge(y_s, w_glu_b, gates, y_a, h, w_out_b, g_x, w_q_b, i, nb, tm)
        o = _attention(q, kv, i, ts)
        j = i // 2
        if i % 2 == 0:
            h = _dense_ffn(o, h, w_o_b, g_ffn, ffn_gu_b, ffn_d_b, i, j, tm)
            kind, stream = "rows", (h,)
        else:
            h, hn, wts, route, cnt = _router(o, h, w_o_b, g_ffn, rw_pad[j], rb_pad[j:j + 1], tri, i, tm)
            row_token, pos, tile_expert, tile_valid = _dispatch_tables(route, cnt, tm_ff)
            rows_of = lambda a, ix: a.at[ix].get(mode="promise_in_bounds")
            ntiles = tile_expert.shape[0]
            bounds = sorted({0, ntiles // 36, ntiles // 9, ntiles // 3, (2 * ntiles) // 3, ntiles})
            yg = None
            for lo, hi in zip(bounds[:-1], bounds[1:]):
                xg = rows_of(hn, row_token[lo * tm_ff:hi * tm_ff])
                yg = _moe_experts(xg, moe_w_gate_up, moe_w_down, tile_expert, tile_valid, j, tm_ff,
                                  lo, ntiles * tm_ff, yg)
            kind, stream = "moe", (h, rows_of(yg, pos[0]), rows_of(yg, pos[1]), wts)

    return _final_norm(kind, stream, final_norm_g.reshape(1, d), nb, tm)
```

```python
import functools
import math

import jax
import jax.numpy as jnp
from jax import lax
from jax.experimental import pallas as pl
from jax.experimental.pallas import tpu as pltpu

F32 = jnp.float32
BF16 = jnp.bfloat16

RMS_EPS = 1e-6
LN_EPS = 1e-5
CONV_KERNEL = 31
SSM_GROUP = 16
SSM_STATE = 64
XATTN_HEADS = 4
N_EXPERTS = 8
TOP_K = 2

SUBLANES = 8
LANES = 128
HALO_BLOCK = 256
ROW_PART = 256
VMEM_LIMIT = 52 * 2 ** 20


def _params(*sem):
    return pltpu.CompilerParams(dimension_semantics=sem, vmem_limit_bytes=VMEM_LIMIT)


def _rms(x, g):
    return x * lax.rsqrt(jnp.mean(x * x, axis=-1, keepdims=True) + RMS_EPS) * g


def _dot(a, b):
    return jnp.dot(a, b, preferred_element_type=F32)


def _moe_combine(h_ref, y0_ref, y1_ref, wt_ref, rows=slice(None)):
    return h_ref[rows, :] + wt_ref[rows, 0:1] * y0_ref[rows, :] + wt_ref[rows, 1:2] * y1_ref[rows, :]


def _row_parts(tm, nsplit):
    sub = tm // nsplit
    return [slice(p * sub, (p + 1) * sub) for p in range(nsplit)]


def _in_kernel(*refs, kind, cw, sw, nsplit):
    if kind == "rows":
        h_ref, g_ref, w_ref, uc_ref, us_ref, gt_ref = refs
        x_of = lambda r: h_ref[r, :]
    elif kind == "batches":
        h_ref, g_ref, w_ref, uc_ref, us_ref, gt_ref, hrow_ref, slab_ref = refs
        hrow_ref[...] = _rows_from_batches(h_ref, slab_ref)
        x_of = lambda r: hrow_ref[r, :]
    else:
        h_ref, y0_ref, y1_ref, wt_ref, g_ref, w_ref, uc_ref, us_ref, gt_ref, hrow_ref = refs

        def x_of(r):
            x = _moe_combine(h_ref, y0_ref, y1_ref, wt_ref, r)
            hrow_ref[r, :] = x
            return x

    for r in _row_parts(uc_ref.shape[0], nsplit):
        xn = _rms(x_of(r), g_ref[...]).astype(BF16)
        a = _dot(xn, w_ref[:, 0:cw])
        gate = _dot(xn, w_ref[:, cw:2 * cw])
        uc_ref[r, :] = a * jax.nn.sigmoid(gate)
        us_ref[r, :] = _dot(xn, w_ref[:, 2 * cw:2 * cw + sw])
        gt_ref[r, :] = jax.nn.sigmoid(_dot(xn, w_ref[:, 2 * cw + sw:])).astype(BF16)


def _stream_specs(kind, stream, tm):
    d = stream[0].shape[-1]
    row = pl.BlockSpec((tm, d), lambda i: (i, 0))
    if kind == "rows":
        return [row]
    if kind == "batches":
        nb = stream[0].shape[0]
        return [pl.BlockSpec((nb, tm // nb, d), lambda i: (0, i, 0))]
    return [row, row, row, pl.BlockSpec((tm, LANES), lambda i: (i, 0))]


def _in_proj(kind, stream, g, w, layer, cw, sw, tm):
    d = stream[0].shape[-1]
    t = stream[0].size // d
    cols = w.shape[-1]
    ng = cols - 2 * cw - sw
    out_specs = [pl.BlockSpec((tm, cw), lambda i: (i, 0)),
                 pl.BlockSpec((tm, sw), lambda i: (i, 0)),
                 pl.BlockSpec((tm, ng), lambda i: (i, 0))]
    out_shape = [jax.ShapeDtypeStruct((t, cw), F32),
                 jax.ShapeDtypeStruct((t, sw), F32),
                 jax.ShapeDtypeStruct((t, ng), BF16)]
    scratch = []
    if kind != "rows":
        out_specs.append(pl.BlockSpec((tm, d), lambda i: (i, 0)))
        out_shape.append(jax.ShapeDtypeStruct((t, d), F32))
    if kind == "batches":
        scratch.append(pltpu.VMEM((d // LANES, tm, LANES), F32))
    return pl.pallas_call(
        functools.partial(_in_kernel, kind=kind, cw=cw, sw=sw, nsplit=tm // ROW_PART),
        grid=(t // tm,),
        in_specs=_stream_specs(kind, stream, tm) + [
            pl.BlockSpec((None, 1, d), lambda i: (layer, 0, 0)),
            pl.BlockSpec((None, d, cols), lambda i: (layer, 0, 0), pipeline_mode=pl.Buffered(1))],
        out_specs=out_specs,
        out_shape=out_shape,
        scratch_shapes=scratch,
        compiler_params=_params("parallel"),
        name="in_proj",
    )(*stream, g, w)


def _conv_kernel(cur_ref, prev_ref, cw_ref, cb_ref, lg_ref, lb_ref, wpw_ref, out_ref,
                 ext_ref, conv_ref, act_ref, *, rows_per_chunk):
    i = pl.program_id(0)
    tm, c = cur_ref.shape
    nslab = c // LANES
    prev = jnp.where(i > 0, prev_ref[...], 0.0)
    for j in range(nslab):
        ext_ref[j, 0:HALO_BLOCK, :] = prev[:, j * LANES:(j + 1) * LANES]
        ext_ref[j, HALO_BLOCK:, :] = cur_ref[:, j * LANES:(j + 1) * LANES]
    first = HALO_BLOCK - (CONV_KERNEL - 1) * SUBLANES
    rc = rows_per_chunk
    rcb = rc
    reps = rcb // SUBLANES

    def conv_block(n, carry):
        ci = n // nslab
        j = n % nslab
        r0 = pl.multiple_of(ci * rcb, rcb)
        accs = [cb_ref[j]] * reps
        for k in range(CONV_KERNEL):
            w = cw_ref[k * nslab + j]
            accs = [a + w * ext_ref[j, pl.ds(r0 + first + SUBLANES * (k + r), SUBLANES), :]
                    for r, a in enumerate(accs)]
        conv_ref[j, pl.ds(r0, rcb), :] = jnp.concatenate(accs, axis=0)
        return carry

    lax.fori_loop(0, (tm // rcb) * nslab, conv_block, 0)

    def norm_chunk(ci, carry):
        r0 = pl.multiple_of(ci * rc, rc)
        acc = jnp.concatenate([conv_ref[j, pl.ds(r0, rc), :] for j in range(nslab)], axis=-1)
        mu = jnp.mean(acc, axis=-1, keepdims=True)
        xc = acc - mu
        y = xc * lax.rsqrt(jnp.mean(xc * xc, axis=-1, keepdims=True) + LN_EPS)
        y = y * lg_ref[...] + lb_ref[...]
        act_ref[pl.ds(r0, rc), :] = jax.nn.silu(y).astype(BF16)
        return carry

    lax.fori_loop(0, tm // rc, norm_chunk, 0, unroll=4)
    out_ref[...] = _dot(act_ref[...], wpw_ref[...]).astype(BF16)


def _conv_branch(u, conv_w, conv_b, ln_g, ln_b, w_pw, layer, tm):
    t, c = u.shape
    d = w_pw.shape[-1]
    ratio = tm // HALO_BLOCK
    return pl.pallas_call(
        functools.partial(_conv_kernel, rows_per_chunk=64),
        grid=(t // tm,),
        in_specs=[pl.BlockSpec((tm, c), lambda i: (i, 0)),
                  pl.BlockSpec((HALO_BLOCK, c), lambda i: (jnp.maximum(i * ratio - 1, 0), 0)),
                  pl.BlockSpec((None,) + conv_w.shape[1:], lambda i: (layer, 0, 0, 0)),
                  pl.BlockSpec((None,) + conv_b.shape[1:], lambda i: (layer, 0, 0, 0)),
                  pl.BlockSpec((None, 1, c), lambda i: (layer, 0, 0)),
                  pl.BlockSpec((None, 1, c), lambda i: (layer, 0, 0)),
                  pl.BlockSpec((None, c, d), lambda i: (layer, 0, 0))],
        out_specs=pl.BlockSpec((tm, d), lambda i: (i, 0)),
        out_shape=jax.ShapeDtypeStruct((t, d), BF16),
        scratch_shapes=[pltpu.VMEM((c // LANES, tm + HALO_BLOCK, LANES), F32),
                        pltpu.VMEM((c // LANES, tm, LANES), F32), pltpu.VMEM((tm, c), BF16)],
        compiler_params=_params("parallel"),
        name="conv_branch",
    )(u, u, conv_w, conv_b, ln_g, ln_b, w_pw)


def _ssm_kernel(u_ref, bm_ref, cm_ref, a_ref, d_ref, out_ref, hs_ref, st_ref, *, nsplit):
    step_idx = pl.program_id(0)
    rows, width = u_ref.shape
    hw = width // 2
    hs_cols = hs_ref.shape[1] // 2
    hstates = hs_cols // 2
    sub = rows // nsplit

    @pl.when(step_idx == 0)
    def _():
        st_ref[...] = jnp.zeros_like(st_ref)

    cols = [(hf * hs_cols + j * LANES, hf * hs_cols + hstates + j * LANES, hf * hstates + j * LANES)
            for hf in range(2) for j in range(hstates // LANES)]

    def input_matmul(p):
        part = slice(p * sub, (p + 1) * sub)
        for hf in range(2):
            ub = u_ref[part, hf * hw:(hf + 1) * hw].astype(BF16)
            hs_ref[part, hf * hs_cols:(hf + 1) * hs_cols] = _dot(ub, bm_ref[hf])

    def scan(p):
        for cr, ci, ac in cols:
            ar = a_ref[0:SUBLANES, ac:ac + LANES]
            ai = a_ref[SUBLANES:2 * SUBLANES, ac:ac + LANES]
            hr = st_ref[:, cr:cr + LANES]
            hi = st_ref[:, ci:ci + LANES]
            for t in range(sub // SUBLANES):
                step = slice(p * sub + t * SUBLANES, p * sub + (t + 1) * SUBLANES)
                bur = hs_ref[step, cr:cr + LANES]
                bui = hs_ref[step, ci:ci + LANES]
                hr, hi = ar * hr - ai * hi + bur, ar * hi + ai * hr + bui
                hs_ref[step, cr:cr + LANES] = hr
                hs_ref[step, ci:ci + LANES] = hi
            st_ref[:, cr:cr + LANES] = hr
            st_ref[:, ci:ci + LANES] = hi

    def output_matmul(p):
        part = slice(p * sub, (p + 1) * sub)
        for hf in range(2):
            half = slice(hf * hw, (hf + 1) * hw)
            hb = hs_ref[part, hf * hs_cols:(hf + 1) * hs_cols].astype(BF16)
            y = _dot(hb, cm_ref[hf]) + d_ref[:, half] * u_ref[part, half]
            out_ref[part, half] = jax.nn.gelu(y).astype(BF16)

    input_matmul(0)
    for p in range(nsplit):
        if p + 1 < nsplit:
            input_matmul(p + 1)
        scan(p)
        output_matmul(p)


def _ssm_branch(u, bm, cm, a_rows, d_row, rows):
    t, width = u.shape
    ncols = 2 * bm.shape[-1]
    return pl.pallas_call(
        functools.partial(_ssm_kernel, nsplit=2),
        grid=(t // rows,),
        in_specs=[pl.BlockSpec((rows, width), lambda i: (i, 0)),
                  pl.BlockSpec(bm.shape, lambda i: (0, 0, 0)),
                  pl.BlockSpec(cm.shape, lambda i: (0, 0, 0)),
                  pl.BlockSpec(a_rows.shape, lambda i: (0, 0)),
                  pl.BlockSpec(d_row.shape, lambda i: (0, 0))],
        out_specs=pl.BlockSpec((rows, width), lambda i: (i, 0)),
        out_shape=jax.ShapeDtypeStruct((t, width), BF16),
        scratch_shapes=[pltpu.VMEM((rows, ncols), F32), pltpu.VMEM((SUBLANES, ncols), F32)],
        compiler_params=_params("arbitrary"),
        name="ssm_branch",
    )(u, bm, cm, a_rows, d_row)


def _ssm_tables(lam_re, lam_im, log_dt, b_re, b_im, c_re, c_im, d_skip):
    g, p = lam_re.shape
    gh = g // 2
    dt = jnp.exp(log_dt)[:, None]
    mag = jnp.exp(lam_re * dt)
    ar = mag * jnp.cos(lam_im * dt)
    ai = mag * jnp.sin(lam_im * dt)
    den = lam_re * lam_re + lam_im * lam_im
    xr = ar - 1.0
    kr = (xr * lam_re + ai * lam_im) / den
    ki = (ai * lam_re - xr * lam_im) / den
    bbar_r = kr[..., None] * b_re - ki[..., None] * b_im
    bbar_i = kr[..., None] * b_im + ki[..., None] * b_re
    eye = jnp.eye(gh, dtype=F32)

    def bd_in(m):
        m = m.reshape(2, gh, p, SSM_GROUP)
        return jnp.einsum("aqph,qr->aqhrp", m, eye).reshape(2, gh * SSM_GROUP, gh * p)

    def bd_out(m):
        m = m.reshape(2, gh, SSM_GROUP, p)
        return jnp.einsum("aqhp,qr->aqprh", m, eye).reshape(2, gh * p, gh * SSM_GROUP)

    bm = jnp.concatenate([bd_in(bbar_r), bd_in(bbar_i)], axis=-1).astype(BF16)
    cm = jnp.concatenate([bd_out(c_re), -bd_out(c_im)], axis=1).astype(BF16)
    a_rows = jnp.concatenate([jnp.broadcast_to(ar.reshape(1, g * p), (SUBLANES, g * p)),
                              jnp.broadcast_to(ai.reshape(1, g * p), (SUBLANES, g * p))], axis=0)
    return bm, cm, a_rows, d_skip.reshape(1, g * SSM_GROUP)


def _slabs_store(val, slab_ref, rows=slice(None)):
    for j in range(slab_ref.shape[0]):
        slab_ref[j, rows, :] = val[:, j * LANES:(j + 1) * LANES]


def _slabs_to_batches(slab_ref, out_ref):
    nb, n, d = out_ref.shape
    for b in range(nb):
        rows = [slab_ref[j, pl.ds(b, n, stride=nb), :] for j in range(d // LANES)]
        out_ref[b] = jnp.concatenate(rows, axis=-1).astype(out_ref.dtype)


def _batches_from_rows(val, slab_ref, out_ref):
    _slabs_store(val, slab_ref)
    _slabs_to_batches(slab_ref, out_ref)


def _batches_to_slabs(in_ref, slab_ref):
    nb, n, d = in_ref.shape
    for b in range(nb):
        xb = in_ref[b].astype(F32)
        for j in range(d // LANES):
            slab_ref[j, pl.ds(b, n, stride=nb), :] = xb[:, j * LANES:(j + 1) * LANES]


def _slab_rows(slab_ref, rows=slice(None)):
    return jnp.concatenate([slab_ref[j, rows, :] for j in range(slab_ref.shape[0])], axis=-1)


def _rows_from_batches(in_ref, slab_ref):
    _batches_to_slabs(in_ref, slab_ref)
    return _slab_rows(slab_ref)


def _merge_kernel(ys_ref, wglu_ref, gt_ref, ya_ref, h_ref, wout_ref, gx_ref, wq_ref, hout_ref, q_ref,
                  slab_ref, *, nsplit):
    tm, d = h_ref.shape
    for r in _row_parts(tm, nsplit):
        z = _dot(ys_ref[r, :], wglu_ref[...])
        yb = z[:, :d] * jax.nn.sigmoid(z[:, d:])
        m = gt_ref[r, :d].astype(F32) * ya_ref[r, :].astype(F32) + gt_ref[r, d:].astype(F32) * yb
        h2 = h_ref[r, :] + _dot(m.astype(BF16), wout_ref[...])
        hout_ref[r, :] = h2
        _slabs_store(_dot(_rms(h2, gx_ref[...]).astype(BF16), wq_ref[...]), slab_ref, r)
    _slabs_to_batches(slab_ref, q_ref)


def _merge(ys, w_glu, gates, ya, h, w_out, gx, w_q, layer, nb, tm):
    t, d = h.shape
    sw = ys.shape[1]
    n = tm // nb
    return pl.pallas_call(
        functools.partial(_merge_kernel, nsplit=1),
        grid=(t // tm,),
        in_specs=[pl.BlockSpec((tm, sw), lambda i: (i, 0)),
                  pl.BlockSpec((None, sw, 2 * d), lambda i: (layer, 0, 0)),
                  pl.BlockSpec((tm, 2 * d), lambda i: (i, 0)),
                  pl.BlockSpec((tm, d), lambda i: (i, 0)),
                  pl.BlockSpec((tm, d), lambda i: (i, 0)),
                  pl.BlockSpec((None, d, d), lambda i: (layer, 0, 0)),
                  pl.BlockSpec((None, 1, d), lambda i: (layer, 0, 0)),
                  pl.BlockSpec((None, d, d), lambda i: (layer, 0, 0))],
        out_specs=[pl.BlockSpec((tm, d), lambda i: (i, 0)),
                   pl.BlockSpec((nb, n, d), lambda i: (0, i, 0))],
        out_shape=[jax.ShapeDtypeStruct((t, d), F32), jax.ShapeDtypeStruct((nb, t // nb, d), BF16)],
        scratch_shapes=[pltpu.VMEM((d // LANES, tm, LANES), F32)],
        compiler_params=_params("parallel"),
        name="merge",
    )(ys, w_glu, gates, ya, h, w_out, gx, w_q)


def _kv_kernel(mem_ref, g_ref, w_ref, out_ref):
    xn = _rms(mem_ref[...], g_ref[...]).astype(BF16)
    out_ref[...] = _dot(xn, w_ref[...]).astype(BF16)


def _kv_proj(mem2d, g, w_kv):
    nl, d, d2 = w_kv.shape
    bm = mem2d.shape[0]
    return pl.pallas_call(
        _kv_kernel,
        grid=(nl, d2 // d),
        in_specs=[pl.BlockSpec((bm, d), lambda l, j: (0, 0)),
                  pl.BlockSpec((1, d), lambda l, j: (0, 0)),
                  pl.BlockSpec((None, d, d), lambda l, j: (l, 0, j))],
        out_specs=pl.BlockSpec((None, bm, d), lambda l, j: (l, 0, j)),
        out_shape=jax.ShapeDtypeStruct((nl, bm, d2), BF16),
        compiler_params=_params("parallel", "parallel"),
        name="kv_proj",
    )(mem2d, g, w_kv)


def _attn_kernel(q_ref, kv_ref, out_ref):
    d = q_ref.shape[1]
    hd = d // XATTN_HEADS
    scale = 1.0 / math.sqrt(hd)
    for n in range(XATTN_HEADS):
        qh = q_ref[:, n * hd:(n + 1) * hd]
        kh = kv_ref[:, n * hd:(n + 1) * hd]
        vh = kv_ref[:, d + n * hd:d + (n + 1) * hd]
        s = lax.dot_general(qh, kh, (((1,), (1,)), ((), ())), preferred_element_type=F32) * scale
        e = jnp.exp(s - jnp.max(s, axis=-1, keepdims=True))
        p = e / jnp.sum(e, axis=-1, keepdims=True)
        out_ref[:, n * hd:(n + 1) * hd] = _dot(p.astype(BF16), vh).astype(out_ref.dtype)


def _attention(q, kv, layer, ts):
    nb, s, d = q.shape
    m = kv.shape[1] // nb
    return pl.pallas_call(
        _attn_kernel,
        grid=(nb, s // ts),
        in_specs=[pl.BlockSpec((None, ts, d), lambda b, i: (b, i, 0)),
                  pl.BlockSpec((None, m, 2 * d), lambda b, i: (layer, b, 0))],
        out_specs=pl.BlockSpec((None, ts, d), lambda b, i: (b, i, 0)),
        out_shape=jax.ShapeDtypeStruct((nb, s, d), BF16),
        compiler_params=_params("parallel", "parallel"),
        name="xattn",
    )(q, kv)


def _attn_residual(h_ref, wo_ref, slab_ref, rows=slice(None)):
    return h_ref[rows, :] + _dot(_slab_rows(slab_ref, rows).astype(BF16), wo_ref[...])


def _swiglu_step(xn, wg_ref, wu_ref, wd_ref):
    g = _dot(xn, wg_ref[...].astype(BF16))
    u = _dot(xn, wu_ref[...].astype(BF16))
    return _dot((jax.nn.silu(g) * u).astype(BF16), wd_ref[...].astype(BF16))


def _ffn_kernel(o_ref, h_ref, wo_ref, g_ref, wgu_ref, wd_ref, out_ref, slab_ref, *, tf):
    ff = wd_ref.shape[0]
    _batches_to_slabs(o_ref, slab_ref)
    h2 = _attn_residual(h_ref, wo_ref, slab_ref)
    xn = _rms(h2, g_ref[...]).astype(BF16)
    out_ref[...] = h2
    for c in range(ff // tf):
        g = _dot(xn, wgu_ref[:, c * tf:(c + 1) * tf])
        u = _dot(xn, wgu_ref[:, ff + c * tf:ff + (c + 1) * tf])
        out_ref[...] += _dot((jax.nn.silu(g) * u).astype(BF16), wd_ref[c * tf:(c + 1) * tf, :])


def _ff_tile(ff):
    for tf in (512, 256, 128):
        if ff % tf == 0:
            return tf
    raise ValueError(f"hidden size {ff} is not a multiple of {LANES}")


def _dense_ffn(o, h, w_o, g, w_gu, w_d, layer, j, tm):
    t, d = h.shape
    nb = o.shape[0]
    n = tm // nb
    ff = w_d.shape[1]
    once = dict(pipeline_mode=pl.Buffered(1))
    return pl.pallas_call(
        functools.partial(_ffn_kernel, tf=_ff_tile(ff)),
        grid=(t // tm,),
        in_specs=[pl.BlockSpec((nb, n, d), lambda i: (0, i, 0)),
                  pl.BlockSpec((tm, d), lambda i: (i, 0)),
                  pl.BlockSpec((None, d, d), lambda i: (layer, 0, 0), **once),
                  pl.BlockSpec((None, 1, d), lambda i: (layer, 0, 0)),
                  pl.BlockSpec((None, d, 2 * ff), lambda i: (j, 0, 0), **once),
                  pl.BlockSpec((None, ff, d), lambda i: (j, 0, 0), **once)],
        out_specs=pl.BlockSpec((tm, d), lambda i: (i, 0)),
        out_shape=jax.ShapeDtypeStruct((t, d), F32),
        scratch_shapes=[pltpu.VMEM((d // LANES, tm, LANES), F32)],
        compiler_params=_params("parallel"),
        name="dense_ffn",
    )(o, h, w_o, g, w_gu, w_d)


def _router_kernel(o_ref, h_ref, wo_ref, g_ref, rw_ref, rb_ref, tri_ref, h2_ref, hn_ref, wt_ref, route_ref,
                   cnt_ref, slab_ref, base_ref):
    @pl.when(pl.program_id(0) == 0)
    def _():
        base_ref[...] = jnp.zeros_like(base_ref)

    _batches_to_slabs(o_ref, slab_ref)
    h2 = _attn_residual(h_ref, wo_ref, slab_ref)
    h2_ref[...] = h2
    xn = _rms(h2, g_ref[...])
    xh = xn.astype(BF16)
    hn_ref[...] = xn
    xl = (xn - xh.astype(F32)).astype(BF16)
    ph = _dot(xh, rw_ref[...])
    logits = ph[:, :LANES] + ph[:, LANES:] + _dot(xl, rw_ref[:, :LANES]) + rb_ref[...]
    lane = lax.broadcasted_iota(jnp.int32, logits.shape, 1)
    lanef = lane.astype(F32)
    neg = jnp.float32(-jnp.inf)
    big = jnp.float32(LANES)
    l1 = jnp.where(lane < N_EXPERTS, logits, neg)
    m1 = jnp.max(l1, axis=-1, keepdims=True)
    i1 = jnp.min(jnp.where(l1 == m1, lanef, big), axis=-1, keepdims=True)
    l2 = jnp.where(lanef == i1, neg, l1)
    m2 = jnp.max(l2, axis=-1, keepdims=True)
    i2 = jnp.min(jnp.where(l2 == m2, lanef, big), axis=-1, keepdims=True)
    e = jnp.exp(m2 - m1)
    w1 = 1.0 / (1.0 + e)
    w2 = e / (1.0 + e)
    top = jnp.float32(N_EXPERTS - 1)
    i1, i2 = jnp.minimum(i1, top), jnp.minimum(i2, top)
    wt_ref[...] = jnp.where(lane == 0, w1, jnp.where(lane == 1, w2, 0.0))
    hit1, hit2 = lanef == i1, lanef == i2
    seen = _dot(tri_ref[...], jnp.where(hit1 | hit2, 1.0, 0.0).astype(BF16)) + base_ref[...]
    r1 = jnp.sum(jnp.where(hit1, seen, 0.0), axis=-1, keepdims=True) - 1.0
    r2 = jnp.sum(jnp.where(hit2, seen, 0.0), axis=-1, keepdims=True) - 1.0
    base_ref[...] = seen[seen.shape[0] - 1:, :]
    cnt_ref[...] = base_ref[...]
    packed = jnp.where(lane == 0, i1, jnp.where(lane == 1, i2, jnp.where(lane == 2, r1,
                                                                         jnp.where(lane == 3, r2, 0.0))))
    route_ref[...] = packed.T[:SUBLANES, :].astype(jnp.int32)


def _router(o, h, w_o, g, rw, rb, tri, layer, tm):
    t, d = h.shape
    nb = o.shape[0]
    n = tm // nb
    return pl.pallas_call(
        _router_kernel,
        grid=(t // tm,),
        in_specs=[pl.BlockSpec((nb, n, d), lambda i: (0, i, 0)),
                  pl.BlockSpec((tm, d), lambda i: (i, 0)),
                  pl.BlockSpec((None, d, d), lambda i: (layer, 0, 0)),
                  pl.BlockSpec((None, 1, d), lambda i: (layer, 0, 0)),
                  pl.BlockSpec((d, 2 * LANES), lambda i: (0, 0)),
                  pl.BlockSpec((1, LANES), lambda i: (0, 0)),
                  pl.BlockSpec((tm, tm), lambda i: (0, 0))],
        out_specs=[pl.BlockSpec((tm, d), lambda i: (i, 0)),
                   pl.BlockSpec((tm, d), lambda i: (i, 0)),
                   pl.BlockSpec((tm, LANES), lambda i: (i, 0)),
                   pl.BlockSpec((SUBLANES, tm), lambda i: (0, i)),
                   pl.BlockSpec((1, LANES), lambda i: (0, 0))],
        out_shape=[jax.ShapeDtypeStruct((t, d), F32),
                   jax.ShapeDtypeStruct((t, d), F32),
                   jax.ShapeDtypeStruct((t, LANES), F32),
                   jax.ShapeDtypeStruct((SUBLANES, t), jnp.int32),
                   jax.ShapeDtypeStruct((1, LANES), F32)],
        scratch_shapes=[pltpu.VMEM((d // LANES, tm, LANES), F32), pltpu.VMEM((1, LANES), F32)],
        compiler_params=_params("arbitrary"),
        name="router",
    )(o, h, w_o, g, rw, rb, tri)


def _moe_kernel(te_ref, tv_ref, x_ref, wg_ref, wu_ref, wd_ref, *rest, tile0):
    out_ref, xb_ref, acc_ref = rest[-3:]
    i = tile0 + pl.program_id(0)
    f = pl.program_id(1)
    valid = tv_ref[i] > 0

    @pl.when(f == 0)
    def _():
        xb_ref[...] = x_ref[...].astype(BF16)
        acc_ref[...] = jnp.zeros_like(acc_ref)

    @pl.when(valid)
    def _():
        acc_ref[...] += _swiglu_step(xb_ref[...], wg_ref, wu_ref, wd_ref)

    @pl.when(f == pl.num_programs(1) - 1)
    def _():
        out_ref[...] = acc_ref[...].astype(out_ref.dtype)


def _moe_experts(xg, w_gu, w_d, tile_expert, tile_valid, j, tm, tile0, total_rows, prev):
    r, d = xg.shape
    ff = w_d.shape[2]
    tf = _ff_tile(ff)
    nf = ff // tf
    in_specs = [pl.BlockSpec((tm, d), lambda i, f, te, tv: (i, 0)),
                pl.BlockSpec((None, None, d, tf), lambda i, f, te, tv: (j, te[tile0 + i], 0, f)),
                pl.BlockSpec((None, None, d, tf), lambda i, f, te, tv: (j, te[tile0 + i], 0, nf + f)),
                pl.BlockSpec((None, None, tf, d), lambda i, f, te, tv: (j, te[tile0 + i], f, 0))]
    args = [tile_expert, tile_valid, xg, w_gu, w_gu, w_d]
    aliases = {}
    if prev is not None:
        in_specs.append(pl.BlockSpec(memory_space=pl.ANY))
        aliases = {len(args): 0}
        args.append(prev)
    grid_spec = pltpu.PrefetchScalarGridSpec(
        num_scalar_prefetch=2,
        grid=(r // tm, nf),
        in_specs=in_specs,
        out_specs=pl.BlockSpec((tm, d), lambda i, f, te, tv: (tile0 + i, 0)),
        scratch_shapes=[pltpu.VMEM((tm, d), BF16), pltpu.VMEM((tm, d), F32)])
    return pl.pallas_call(
        functools.partial(_moe_kernel, tile0=tile0),
        grid_spec=grid_spec,
        out_shape=jax.ShapeDtypeStruct((total_rows, d), BF16),
        input_output_aliases=aliases,
        compiler_params=_params("parallel", "arbitrary"),
        name="moe_experts",
    )(*args)


def _dispatch_tables(route, cnt, tm):
    t = route.shape[1]
    npairs = t * TOP_K
    ntiles = npairs // tm + N_EXPERTS
    counts = cnt[0, :N_EXPERTS].astype(jnp.int32)
    padded = ((counts + tm - 1) // tm) * tm
    upto = jnp.tril(jnp.ones((N_EXPERTS, N_EXPERTS), bool))
    pend = jnp.sum(jnp.where(upto, padded[None, :], 0), axis=1)
    pstart = pend - padded
    cstart = jnp.sum(jnp.where(upto, counts[None, :], 0), axis=1) - counts
    experts, ranks = route[0:TOP_K], route[TOP_K:2 * TOP_K]
    pos = ranks
    for e in range(N_EXPERTS):
        pos = pos + jnp.where(experts == e, pstart[e], 0)
    order = jnp.argsort(pos.reshape(npairs))
    tile_start = jnp.arange(ntiles, dtype=jnp.int32) * tm
    tile_expert = jnp.minimum(jnp.searchsorted(pend, tile_start, side="right"),
                              N_EXPERTS - 1).astype(jnp.int32)
    tile_valid = (tile_start < pend[-1]).astype(jnp.int32)
    slot = jnp.arange(ntiles * tm, dtype=jnp.int32)
    slot_e = jnp.repeat(tile_expert, tm)
    within = slot - pstart[slot_e]
    src = jnp.clip(cstart[slot_e] + within, 0, npairs - 1)
    row_token = jnp.where(within < counts[slot_e], order[src] % t, slot % t).astype(jnp.int32)
    return row_token, pos, tile_expert, tile_valid


def _final_kernel(*refs, kind):
    *stream, g_ref, out_ref, slab_ref = refs
    x = stream[0][...] if kind == "rows" else _moe_combine(*stream)
    _batches_from_rows(_rms(x, g_ref[...]), slab_ref, out_ref)


def _final_norm(kind, stream, g, nb, tm):
    t, d = stream[0].shape
    return pl.pallas_call(
        functools.partial(_final_kernel, kind=kind),
        grid=(t // tm,),
        in_specs=_stream_specs(kind, stream, tm) + [pl.BlockSpec((1, d), lambda i: (0, 0))],
        out_specs=pl.BlockSpec((nb, tm // nb, d), lambda i: (0, i, 0)),
        out_shape=jax.ShapeDtypeStruct((nb, t // nb, d), F32),
        scratch_shapes=[pltpu.VMEM((d // LANES, tm, LANES), F32)],
        compiler_params=_params("parallel"),
        name="final_norm",
    )(*stream, g)


def kernel(x, mem, mem_norm_g, norm_mix_g, w_in, conv_w, conv_b, conv_ln_g, conv_ln_b, w_conv_pw,
           ssm_lambda_re, ssm_lambda_im, ssm_log_dt, ssm_b_re, ssm_b_im, ssm_c_re, ssm_c_im, ssm_d,
           w_ssm_glu, w_out, norm_xattn_g, w_q, w_kv, w_o, norm_ffn_g, ffn_w_gate_up, ffn_w_down,
           router_w, router_b, moe_w_gate_up, moe_w_down, final_norm_g):
    nb, s, d = x.shape
    assert nb == SUBLANES, "the row layout puts the batch on the sublane axis"
    depth = w_in.shape[0]
    cw = conv_w.shape[-1]
    sw = ssm_d.shape[1] * ssm_d.shape[2]
    t = nb * s
    tm = min(512, t)
    tm_ff = min(1024, t)
    ts = min(512, s)

    w_in_b = w_in.astype(BF16)
    w_pw_b = w_conv_pw.astype(BF16)
    w_glu_b = w_ssm_glu.astype(BF16)
    w_out_b = w_out.astype(BF16)
    w_q_b = w_q.astype(BF16)
    w_kv_b = w_kv.astype(BF16)
    w_o_b = w_o.astype(BF16)
    ffn_gu_b = ffn_w_gate_up.astype(BF16)
    ffn_d_b = ffn_w_down.astype(BF16)
    nslab = cw // LANES
    conv_w_rep = jnp.broadcast_to(conv_w.reshape(depth, CONV_KERNEL * nslab, 1, LANES),
                                  (depth, CONV_KERNEL * nslab, SUBLANES, LANES))
    conv_b_rep = jnp.broadcast_to(conv_b.reshape(depth, nslab, 1, LANES), (depth, nslab, SUBLANES, LANES))
    row3 = lambda a: a.reshape(a.shape[0], 1, a.shape[1])
    g_mix, g_x, g_ffn = row3(norm_mix_g), row3(norm_xattn_g), row3(norm_ffn_g)
    lg3, lb3 = row3(conv_ln_g), row3(conv_ln_b)
    rw_pad = jnp.pad(router_w, ((0, 0), (0, 0), (0, LANES - N_EXPERTS)))
    rw_hi = rw_pad.astype(BF16)
    rw_pad = jnp.concatenate([rw_hi, (rw_pad - rw_hi.astype(F32)).astype(BF16)], axis=-1)
    rb_pad = jnp.pad(router_b, ((0, 0), (0, LANES - N_EXPERTS)))
    tri = jnp.tril(jnp.ones((tm, tm), BF16))

    kv = _kv_proj(mem.reshape(nb * mem.shape[1], d), mem_norm_g.reshape(1, d), w_kv_b)
    kind, stream = "batches", (x,)

    for i in range(depth):
        tm_in = tm_ff if kind == "rows" else tm
        u_conv, u_ssm, gates, *rows = _in_proj(kind, stream, g_mix, w_in_b, i, cw, sw, tm_in)
        h = rows[0] if rows else stream[0]
        y_a = _conv_branch(u_conv, conv_w_rep, conv_b_rep, lg3, lb3, w_pw_b, i, tm)
        tables = _ssm_tables(ssm_lambda_re[i], ssm_lambda_im[i], ssm_log_dt[i], ssm_b_re[i], ssm_b_im[i],
                             ssm_c_re[i], ssm_c_im[i], ssm_d[i])
        y_s = _ssm_branch(u_ssm, *tables, rows=tm)
        h, q = _merge(y_s, w_glu_b, gates, y_a, h, w_out_b, g_x, w_q_b, i, nb, tm)
        o = _attention(q, kv, i, ts)
        j = i // 2
        if i % 2 == 0:
            h = _dense_ffn(o, h, w_o_b, g_ffn, ffn_gu_b, ffn_d_b, i, j, tm)
            kind, stream = "rows", (h,)
        else:
            h, hn, wts, route, cnt = _router(o, h, w_o_b, g_ffn, rw_pad[j], rb_pad[j:j + 1], tri, i, tm)
            row_token, pos, tile_expert, tile_valid = _dispatch_tables(route, cnt, tm_ff)
            rows_of = lambda a, ix: a.at[ix].get(mode="promise_in_bounds")
            ntiles = tile_expert.shape[0]
            bounds = sorted({0, ntiles // 9, ntiles // 3, (2 * ntiles) // 3, ntiles})
            yg = None
            for lo, hi in zip(bounds[:-1], bounds[1:]):
                xg = rows_of(hn, row_token[lo * tm_ff:hi * tm_ff])
                yg = _moe_experts(xg, moe_w_gate_up, moe_w_down, tile_expert, tile_valid, j, tm_ff,
                                  lo, ntiles * tm_ff, yg)
            kind, stream = "moe", (h, rows_of(yg, pos[0]), rows_of(yg, pos[1]), wts)

    return _final_norm(kind, stream, final_norm_g.reshape(1, d), nb, tm)
```

```python
import functools
import math

import jax
import jax.numpy as jnp
from jax import lax
from jax.experimental import pallas as pl
from jax.experimental.pallas import tpu as pltpu

F32 = jnp.float32
BF16 = jnp.bfloat16

RMS_EPS = 1e-6
LN_EPS = 1e-5
CONV_KERNEL = 31
SSM_GROUP = 16
SSM_STATE = 64
XATTN_HEADS = 4
N_EXPERTS = 8
TOP_K = 2

SUBLANES = 8
LANES = 128
HALO_BLOCK = 256
ROW_PART = 256
VMEM_LIMIT = 52 * 2 ** 20


def _params(*sem):
    return pltpu.CompilerParams(dimension_semantics=sem, vmem_limit_bytes=VMEM_LIMIT)


def _rms(x, g):
    return x * lax.rsqrt(jnp.mean(x * x, axis=-1, keepdims=True) + RMS_EPS) * g


def _dot(a, b):
    return jnp.dot(a, b, preferred_element_type=F32)


def _moe_combine(h_ref, y0_ref, y1_ref, wt_ref, rows=slice(None)):
    return h_ref[rows, :] + wt_ref[rows, 0:1] * y0_ref[rows, :] + wt_ref[rows, 1:2] * y1_ref[rows, :]


def _row_parts(tm, nsplit):
    sub = tm // nsplit
    return [slice(p * sub, (p + 1) * sub) for p in range(nsplit)]


def _in_kernel(*refs, kind, cw, sw, nsplit):
    if kind == "rows":
        h_ref, g_ref, w_ref, uc_ref, us_ref, gt_ref = refs
        x_of = lambda r: h_ref[r, :]
    elif kind == "batches":
        h_ref, g_ref, w_ref, uc_ref, us_ref, gt_ref, hrow_ref, slab_ref = refs
        hrow_ref[...] = _rows_from_batches(h_ref, slab_ref)
        x_of = lambda r: hrow_ref[r, :]
    else:
        h_ref, y0_ref, y1_ref, wt_ref, g_ref, w_ref, uc_ref, us_ref, gt_ref, hrow_ref = refs

        def x_of(r):
            x = _moe_combine(h_ref, y0_ref, y1_ref, wt_ref, r)
            hrow_ref[r, :] = x
            return x

    for r in _row_parts(uc_ref.shape[0], nsplit):
        xn = _rms(x_of(r), g_ref[...]).astype(BF16)
        a = _dot(xn, w_ref[:, 0:cw])
        gate = _dot(xn, w_ref[:, cw:2 * cw])
        uc_ref[r, :] = a * jax.nn.sigmoid(gate)
        us_ref[r, :] = _dot(xn, w_ref[:, 2 * cw:2 * cw + sw])
        gt_ref[r, :] = jax.nn.sigmoid(_dot(xn, w_ref[:, 2 * cw + sw:])).astype(BF16)


def _stream_specs(kind, stream, tm):
    d = stream[0].shape[-1]
    row = pl.BlockSpec((tm, d), lambda i: (i, 0))
    if kind == "rows":
        return [row]
    if kind == "batches":
        nb = stream[0].shape[0]
        return [pl.BlockSpec((nb, tm // nb, d), lambda i: (0, i, 0))]
    return [row, row, row, pl.BlockSpec((tm, LANES), lambda i: (i, 0))]


def _in_proj(kind, stream, g, w, layer, cw, sw, tm):
    d = stream[0].shape[-1]
    t = stream[0].size // d
    cols = w.shape[-1]
    ng = cols - 2 * cw - sw
    out_specs = [pl.BlockSpec((tm, cw), lambda i: (i, 0)),
                 pl.BlockSpec((tm, sw), lambda i: (i, 0)),
                 pl.BlockSpec((tm, ng), lambda i: (i, 0))]
    out_shape = [jax.ShapeDtypeStruct((t, cw), F32),
                 jax.ShapeDtypeStruct((t, sw), F32),
                 jax.ShapeDtypeStruct((t, ng), BF16)]
    scratch = []
    if kind != "rows":
        out_specs.append(pl.BlockSpec((tm, d), lambda i: (i, 0)))
        out_shape.append(jax.ShapeDtypeStruct((t, d), F32))
    if kind == "batches":
        scratch.append(pltpu.VMEM((d // LANES, tm, LANES), F32))
    return pl.pallas_call(
        functools.partial(_in_kernel, kind=kind, cw=cw, sw=sw, nsplit=tm // ROW_PART),
        grid=(t // tm,),
        in_specs=_stream_specs(kind, stream, tm) + [
            pl.BlockSpec((None, 1, d), lambda i: (layer, 0, 0)),
            pl.BlockSpec((None, d, cols), lambda i: (layer, 0, 0), pipeline_mode=pl.Buffered(1))],
        out_specs=out_specs,
        out_shape=out_shape,
        scratch_shapes=scratch,
        compiler_params=_params("parallel"),
        name="in_proj",
    )(*stream, g, w)


def _conv_kernel(cur_ref, prev_ref, cw_ref, cb_ref, lg_ref, lb_ref, wpw_ref, out_ref,
                 ext_ref, conv_ref, act_ref, *, rows_per_chunk):
    i = pl.program_id(0)
    tm, c = cur_ref.shape
    nslab = c // LANES
    prev = jnp.where(i > 0, prev_ref[...], 0.0)
    for j in range(nslab):
        ext_ref[j, 0:HALO_BLOCK, :] = prev[:, j * LANES:(j + 1) * LANES]
        ext_ref[j, HALO_BLOCK:, :] = cur_ref[:, j * LANES:(j + 1) * LANES]
    first = HALO_BLOCK - (CONV_KERNEL - 1) * SUBLANES
    rc = rows_per_chunk
    rcb = rc
    reps = rcb // SUBLANES

    def conv_block(n, carry):
        ci = n // nslab
        j = n % nslab
        r0 = pl.multiple_of(ci * rcb, rcb)
        accs = [cb_ref[j]] * reps
        for k in range(CONV_KERNEL):
            w = cw_ref[k * nslab + j]
            accs = [a + w * ext_ref[j, pl.ds(r0 + first + SUBLANES * (k + r), SUBLANES), :]
                    for r, a in enumerate(accs)]
        conv_ref[j, pl.ds(r0, rcb), :] = jnp.concatenate(accs, axis=0)
        return carry

    lax.fori_loop(0, (tm // rcb) * nslab, conv_block, 0)

    def norm_chunk(ci, carry):
        r0 = pl.multiple_of(ci * rc, rc)
        acc = jnp.concatenate([conv_ref[j, pl.ds(r0, rc), :] for j in range(nslab)], axis=-1)
        mu = jnp.mean(acc, axis=-1, keepdims=True)
        xc = acc - mu
        y = xc * lax.rsqrt(jnp.mean(xc * xc, axis=-1, keepdims=True) + LN_EPS)
        y = y * lg_ref[...] + lb_ref[...]
        act_ref[pl.ds(r0, rc), :] = jax.nn.silu(y).astype(BF16)
        return carry

    lax.fori_loop(0, tm // rc, norm_chunk, 0, unroll=4)
    out_ref[...] = _dot(act_ref[...], wpw_ref[...]).astype(BF16)


def _conv_branch(u, conv_w, conv_b, ln_g, ln_b, w_pw, layer, tm):
    t, c = u.shape
    d = w_pw.shape[-1]
    ratio = tm // HALO_BLOCK
    return pl.pallas_call(
        functools.partial(_conv_kernel, rows_per_chunk=64),
        grid=(t // tm,),
        in_specs=[pl.BlockSpec((tm, c), lambda i: (i, 0)),
                  pl.BlockSpec((HALO_BLOCK, c), lambda i: (jnp.maximum(i * ratio - 1, 0), 0)),
                  pl.BlockSpec((None,) + conv_w.shape[1:], lambda i: (layer, 0, 0, 0)),
                  pl.BlockSpec((None,) + conv_b.shape[1:], lambda i: (layer, 0, 0, 0)),
                  pl.BlockSpec((None, 1, c), lambda i: (layer, 0, 0)),
                  pl.BlockSpec((None, 1, c), lambda i: (layer, 0, 0)),
                  pl.BlockSpec((None, c, d), lambda i: (layer, 0, 0))],
        out_specs=pl.BlockSpec((tm, d), lambda i: (i, 0)),
        out_shape=jax.ShapeDtypeStruct((t, d), BF16),
        scratch_shapes=[pltpu.VMEM((c // LANES, tm + HALO_BLOCK, LANES), F32),
                        pltpu.VMEM((c // LANES, tm, LANES), F32), pltpu.VMEM((tm, c), BF16)],
        compiler_params=_params("parallel"),
        name="conv_branch",
    )(u, u, conv_w, conv_b, ln_g, ln_b, w_pw)


def _ssm_kernel(u_ref, bm_ref, cm_ref, a_ref, d_ref, out_ref, hs_ref, st_ref, *, nsplit):
    step_idx = pl.program_id(0)
    rows, width = u_ref.shape
    hw = width // 2
    hs_cols = hs_ref.shape[1] // 2
    hstates = hs_cols // 2
    sub = rows // nsplit

    @pl.when(step_idx == 0)
    def _():
        st_ref[...] = jnp.zeros_like(st_ref)

    cols = [(hf * hs_cols + j * LANES, hf * hs_cols + hstates + j * LANES, hf * hstates + j * LANES)
            for hf in range(2) for j in range(hstates // LANES)]

    def input_matmul(p):
        part = slice(p * sub, (p + 1) * sub)
        for hf in range(2):
            ub = u_ref[part, hf * hw:(hf + 1) * hw].astype(BF16)
            hs_ref[part, hf * hs_cols:(hf + 1) * hs_cols] = _dot(ub, bm_ref[hf])

    def scan(p):
        for cr, ci, ac in cols:
            ar = a_ref[0:SUBLANES, ac:ac + LANES]
            ai = a_ref[SUBLANES:2 * SUBLANES, ac:ac + LANES]
            hr = st_ref[:, cr:cr + LANES]
            hi = st_ref[:, ci:ci + LANES]
            for t in range(sub // SUBLANES):
                step = slice(p * sub + t * SUBLANES, p * sub + (t + 1) * SUBLANES)
                bur = hs_ref[step, cr:cr + LANES]
                bui = hs_ref[step, ci:ci + LANES]
                hr, hi = ar * hr - ai * hi + bur, ar * hi + ai * hr + bui
                hs_ref[step, cr:cr + LANES] = hr
                hs_ref[step, ci:ci + LANES] = hi
            st_ref[:, cr:cr + LANES] = hr
            st_ref[:, ci:ci + LANES] = hi

    def output_matmul(p):
        part = slice(p * sub, (p + 1) * sub)
        for hf in range(2):
            half = slice(hf * hw, (hf + 1) * hw)
            hb = hs_ref[part, hf * hs_cols:(hf + 1) * hs_cols].astype(BF16)
            y = _dot(hb, cm_ref[hf]) + d_ref[:, half] * u_ref[part, half]
            out_ref[part, half] = jax.nn.gelu(y).astype(BF16)

    input_matmul(0)
    for p in range(nsplit):
        if p + 1 < nsplit:
            input_matmul(p + 1)
        scan(p)
        output_matmul(p)


def _ssm_branch(u, bm, cm, a_rows, d_row, rows):
    t, width = u.shape
    ncols = 2 * bm.shape[-1]
    return pl.pallas_call(
        functools.partial(_ssm_kernel, nsplit=2),
        grid=(t // rows,),
        in_specs=[pl.BlockSpec((rows, width), lambda i: (i, 0)),
                  pl.BlockSpec(bm.shape, lambda i: (0, 0, 0)),
                  pl.BlockSpec(cm.shape, lambda i: (0, 0, 0)),
                  pl.BlockSpec(a_rows.shape, lambda i: (0, 0)),
                  pl.BlockSpec(d_row.shape, lambda i: (0, 0))],
        out_specs=pl.BlockSpec((rows, width), lambda i: (i, 0)),
        out_shape=jax.ShapeDtypeStruct((t, width), BF16),
        scratch_shapes=[pltpu.VMEM((rows, ncols), F32), pltpu.VMEM((SUBLANES, ncols), F32)],
        compiler_params=_params("arbitrary"),
        name="ssm_branch",
    )(u, bm, cm, a_rows, d_row)


def _ssm_tables(lam_re, lam_im, log_dt, b_re, b_im, c_re, c_im, d_skip):
    g, p = lam_re.shape
    gh = g // 2
    dt = jnp.exp(log_dt)[:, None]
    mag = jnp.exp(lam_re * dt)
    ar = mag * jnp.cos(lam_im * dt)
    ai = mag * jnp.sin(lam_im * dt)
    den = lam_re * lam_re + lam_im * lam_im
    xr = ar - 1.0
    kr = (xr * lam_re + ai * lam_im) / den
    ki = (ai * lam_re - xr * lam_im) / den
    bbar_r = kr[..., None] * b_re - ki[..., None] * b_im
    bbar_i = kr[..., None] * b_im + ki[..., None] * b_re
    eye = jnp.eye(gh, dtype=F32)

    def bd_in(m):
        m = m.reshape(2, gh, p, SSM_GROUP)
        return jnp.einsum("aqph,qr->aqhrp", m, eye).reshape(2, gh * SSM_GROUP, gh * p)

    def bd_out(m):
        m = m.reshape(2, gh, SSM_GROUP, p)
        return jnp.einsum("aqhp,qr->aqprh", m, eye).reshape(2, gh * p, gh * SSM_GROUP)

    bm = jnp.concatenate([bd_in(bbar_r), bd_in(bbar_i)], axis=-1).astype(BF16)
    cm = jnp.concatenate([bd_out(c_re), -bd_out(c_im)], axis=1).astype(BF16)
    a_rows = jnp.concatenate([jnp.broadcast_to(ar.reshape(1, g * p), (SUBLANES, g * p)),
                              jnp.broadcast_to(ai.reshape(1, g * p), (SUBLANES, g * p))], axis=0)
    return bm, cm, a_rows, d_skip.reshape(1, g * SSM_GROUP)


def _slabs_store(val, slab_ref, rows=slice(None)):
    for j in range(slab_ref.shape[0]):
        slab_ref[j, rows, :] = val[:, j * LANES:(j + 1) * LANES]


def _slabs_to_batches(slab_ref, out_ref):
    nb, n, d = out_ref.shape
    for b in range(nb):
        rows = [slab_ref[j, pl.ds(b, n, stride=nb), :] for j in range(d // LANES)]
        out_ref[b] = jnp.concatenate(rows, axis=-1).astype(out_ref.dtype)


def _batches_from_rows(val, slab_ref, out_ref):
    _slabs_store(val, slab_ref)
    _slabs_to_batches(slab_ref, out_ref)


def _batches_to_slabs(in_ref, slab_ref):
    nb, n, d = in_ref.shape
    for b in range(nb):
        xb = in_ref[b].astype(F32)
        for j in range(d // LANES):
            slab_ref[j, pl.ds(b, n, stride=nb), :] = xb[:, j * LANES:(j + 1) * LANES]


def _slab_rows(slab_ref, rows=slice(None)):
    return jnp.concatenate([slab_ref[j, rows, :] for j in range(slab_ref.shape[0])], axis=-1)


def _rows_from_batches(in_ref, slab_ref):
    _batches_to_slabs(in_ref, slab_ref)
    return _slab_rows(slab_ref)


def _merge_kernel(ys_ref, wglu_ref, gt_ref, ya_ref, h_ref, wout_ref, gx_ref, wq_ref, hout_ref, q_ref,
                  slab_ref, *, nsplit):
    tm, d = h_ref.shape
    for r in _row_parts(tm, nsplit):
        z = _dot(ys_ref[r, :], wglu_ref[...])
        yb = z[:, :d] * jax.nn.sigmoid(z[:, d:])
        m = gt_ref[r, :d].astype(F32) * ya_ref[r, :].astype(F32) + gt_ref[r, d:].astype(F32) * yb
        h2 = h_ref[r, :] + _dot(m.astype(BF16), wout_ref[...])
        hout_ref[r, :] = h2
        _slabs_store(_dot(_rms(h2, gx_ref[...]).astype(BF16), wq_ref[...]), slab_ref, r)
    _slabs_to_batches(slab_ref, q_ref)


def _merge(ys, w_glu, gates, ya, h, w_out, gx, w_q, layer, nb, tm):
    t, d = h.shape
    sw = ys.shape[1]
    n = tm // nb
    return pl.pallas_call(
        functools.partial(_merge_kernel, nsplit=1),
        grid=(t // tm,),
        in_specs=[pl.BlockSpec((tm, sw), lambda i: (i, 0)),
                  pl.BlockSpec((None, sw, 2 * d), lambda i: (layer, 0, 0)),
                  pl.BlockSpec((tm, 2 * d), lambda i: (i, 0)),
                  pl.BlockSpec((tm, d), lambda i: (i, 0)),
                  pl.BlockSpec((tm, d), lambda i: (i, 0)),
                  pl.BlockSpec((None, d, d), lambda i: (layer, 0, 0)),
                  pl.BlockSpec((None, 1, d), lambda i: (layer, 0, 0)),
                  pl.BlockSpec((None, d, d), lambda i: (layer, 0, 0))],
        out_specs=[pl.BlockSpec((tm, d), lambda i: (i, 0)),
                   pl.BlockSpec((nb, n, d), lambda i: (0, i, 0))],
        out_shape=[jax.ShapeDtypeStruct((t, d), F32), jax.ShapeDtypeStruct((nb, t // nb, d), BF16)],
        scratch_shapes=[pltpu.VMEM((d // LANES, tm, LANES), F32)],
        compiler_params=_params("parallel"),
        name="merge",
    )(ys, w_glu, gates, ya, h, w_out, gx, w_q)


def _kv_kernel(mem_ref, g_ref, w_ref, out_ref):
    xn = _rms(mem_ref[...], g_ref[...]).astype(BF16)
    out_ref[...] = _dot(xn, w_ref[...]).astype(BF16)


def _kv_proj(mem2d, g, w_kv):
    nl, d, d2 = w_kv.shape
    bm = mem2d.shape[0]
    return pl.pallas_call(
        _kv_kernel,
        grid=(nl, d2 // d),
        in_specs=[pl.BlockSpec((bm, d), lambda l, j: (0, 0)),
                  pl.BlockSpec((1, d), lambda l, j: (0, 0)),
                  pl.BlockSpec((None, d, d), lambda l, j: (l, 0, j))],
        out_specs=pl.BlockSpec((None, bm, d), lambda l, j: (l, 0, j)),
        out_shape=jax.ShapeDtypeStruct((nl, bm, d2), BF16),
        compiler_params=_params("parallel", "parallel"),
        name="kv_proj",
    )(mem2d, g, w_kv)


def _attn_kernel(q_ref, kv_ref, out_ref):
    d = q_ref.shape[1]
    hd = d // XATTN_HEADS
    scale = 1.0 / math.sqrt(hd)
    for n in range(XATTN_HEADS):
        qh = q_ref[:, n * hd:(n + 1) * hd]
        kh = kv_ref[:, n * hd:(n + 1) * hd]
        vh = kv_ref[:, d + n * hd:d + (n + 1) * hd]
        s = lax.dot_general(qh, kh, (((1,), (1,)), ((), ())), preferred_element_type=F32) * scale
        e = jnp.exp(s - jnp.max(s, axis=-1, keepdims=True))
        p = e / jnp.sum(e, axis=-1, keepdims=True)
        out_ref[:, n * hd:(n + 1) * hd] = _dot(p.astype(BF16), vh).astype(out_ref.dtype)


def _attention(q, kv, layer, ts):
    nb, s, d = q.shape
    m = kv.shape[1] // nb
    return pl.pallas_call(
        _attn_kernel,
        grid=(nb, s // ts),
        in_specs=[pl.BlockSpec((None, ts, d), lambda b, i: (b, i, 0)),
                  pl.BlockSpec((None, m, 2 * d), lambda b, i: (layer, b, 0))],
        out_specs=pl.BlockSpec((None, ts, d), lambda b, i: (b, i, 0)),
        out_shape=jax.ShapeDtypeStruct((nb, s, d), BF16),
        compiler_params=_params("parallel", "parallel"),
        name="xattn",
    )(q, kv)


def _attn_residual(h_ref, wo_ref, slab_ref, rows=slice(None)):
    return h_ref[rows, :] + _dot(_slab_rows(slab_ref, rows).astype(BF16), wo_ref[...])


def _swiglu_step(xn, wg_ref, wu_ref, wd_ref):
    g = _dot(xn, wg_ref[...].astype(BF16))
    u = _dot(xn, wu_ref[...].astype(BF16))
    return _dot((jax.nn.silu(g) * u).astype(BF16), wd_ref[...].astype(BF16))


def _ffn_kernel(o_ref, h_ref, wo_ref, g_ref, wgu_ref, wd_ref, out_ref, slab_ref, *, tf):
    ff = wd_ref.shape[0]
    _batches_to_slabs(o_ref, slab_ref)
    h2 = _attn_residual(h_ref, wo_ref, slab_ref)
    xn = _rms(h2, g_ref[...]).astype(BF16)
    out_ref[...] = h2
    for c in range(ff // tf):
        g = _dot(xn, wgu_ref[:, c * tf:(c + 1) * tf])
        u = _dot(xn, wgu_ref[:, ff + c * tf:ff + (c + 1) * tf])
        out_ref[...] += _dot((jax.nn.silu(g) * u).astype(BF16), wd_ref[c * tf:(c + 1) * tf, :])


def _ff_tile(ff):
    for tf in (512, 256, 128):
        if ff % tf == 0:
            return tf
    raise ValueError(f"hidden size {ff} is not a multiple of {LANES}")


def _dense_ffn(o, h, w_o, g, w_gu, w_d, layer, j, tm):
    t, d = h.shape
    nb = o.shape[0]
    n = tm // nb
    ff = w_d.shape[1]
    once = dict(pipeline_mode=pl.Buffered(1))
    return pl.pallas_call(
        functools.partial(_ffn_kernel, tf=_ff_tile(ff)),
        grid=(t // tm,),
        in_specs=[pl.BlockSpec((nb, n, d), lambda i: (0, i, 0)),
                  pl.BlockSpec((tm, d), lambda i: (i, 0)),
                  pl.BlockSpec((None, d, d), lambda i: (layer, 0, 0), **once),
                  pl.BlockSpec((None, 1, d), lambda i: (layer, 0, 0)),
                  pl.BlockSpec((None, d, 2 * ff), lambda i: (j, 0, 0), **once),
                  pl.BlockSpec((None, ff, d), lambda i: (j, 0, 0), **once)],
        out_specs=pl.BlockSpec((tm, d), lambda i: (i, 0)),
        out_shape=jax.ShapeDtypeStruct((t, d), F32),
        scratch_shapes=[pltpu.VMEM((d // LANES, tm, LANES), F32)],
        compiler_params=_params("parallel"),
        name="dense_ffn",
    )(o, h, w_o, g, w_gu, w_d)


def _router_kernel(o_ref, h_ref, wo_ref, g_ref, rw_ref, rb_ref, tri_ref, h2_ref, hn_ref, wt_ref, route_ref,
                   cnt_ref, slab_ref, base_ref):
    @pl.when(pl.program_id(0) == 0)
    def _():
        base_ref[...] = jnp.zeros_like(base_ref)

    _batches_to_slabs(o_ref, slab_ref)
    h2 = _attn_residual(h_ref, wo_ref, slab_ref)
    h2_ref[...] = h2
    xn = _rms(h2, g_ref[...])
    xh = xn.astype(BF16)
    hn_ref[...] = xh
    xl = (xn - xh.astype(F32)).astype(BF16)
    ph = _dot(xh, rw_ref[...])
    logits = ph[:, :LANES] + ph[:, LANES:] + _dot(xl, rw_ref[:, :LANES]) + rb_ref[...]
    lane = lax.broadcasted_iota(jnp.int32, logits.shape, 1)
    lanef = lane.astype(F32)
    neg = jnp.float32(-jnp.inf)
    big = jnp.float32(LANES)
    l1 = jnp.where(lane < N_EXPERTS, logits, neg)
    m1 = jnp.max(l1, axis=-1, keepdims=True)
    i1 = jnp.min(jnp.where(l1 == m1, lanef, big), axis=-1, keepdims=True)
    l2 = jnp.where(lanef == i1, neg, l1)
    m2 = jnp.max(l2, axis=-1, keepdims=True)
    i2 = jnp.min(jnp.where(l2 == m2, lanef, big), axis=-1, keepdims=True)
    e = jnp.exp(m2 - m1)
    w1 = 1.0 / (1.0 + e)
    w2 = e / (1.0 + e)
    top = jnp.float32(N_EXPERTS - 1)
    i1, i2 = jnp.minimum(i1, top), jnp.minimum(i2, top)
    wt_ref[...] = jnp.where(lane == 0, w1, jnp.where(lane == 1, w2, 0.0))
    hit1, hit2 = lanef == i1, lanef == i2
    seen = _dot(tri_ref[...], jnp.where(hit1 | hit2, 1.0, 0.0).astype(BF16)) + base_ref[...]
    r1 = jnp.sum(jnp.where(hit1, seen, 0.0), axis=-1, keepdims=True) - 1.0
    r2 = jnp.sum(jnp.where(hit2, seen, 0.0), axis=-1, keepdims=True) - 1.0
    base_ref[...] = seen[seen.shape[0] - 1:, :]
    cnt_ref[...] = base_ref[...]
    packed = jnp.where(lane == 0, i1, jnp.where(lane == 1, i2, jnp.where(lane == 2, r1,
                                                                         jnp.where(lane == 3, r2, 0.0))))
    route_ref[...] = packed.T[:SUBLANES, :].astype(jnp.int32)


def _router(o, h, w_o, g, rw, rb, tri, layer, tm):
    t, d = h.shape
    nb = o.shape[0]
    n = tm // nb
    return pl.pallas_call(
        _router_kernel,
        grid=(t // tm,),
        in_specs=[pl.BlockSpec((nb, n, d), lambda i: (0, i, 0)),
                  pl.BlockSpec((tm, d), lambda i: (i, 0)),
                  pl.BlockSpec((None, d, d), lambda i: (layer, 0, 0)),
                  pl.BlockSpec((None, 1, d), lambda i: (layer, 0, 0)),
                  pl.BlockSpec((d, 2 * LANES), lambda i: (0, 0)),
                  pl.BlockSpec((1, LANES), lambda i: (0, 0)),
                  pl.BlockSpec((tm, tm), lambda i: (0, 0))],
        out_specs=[pl.BlockSpec((tm, d), lambda i: (i, 0)),
                   pl.BlockSpec((tm, d), lambda i: (i, 0)),
                   pl.BlockSpec((tm, LANES), lambda i: (i, 0)),
                   pl.BlockSpec((SUBLANES, tm), lambda i: (0, i)),
                   pl.BlockSpec((1, LANES), lambda i: (0, 0))],
        out_shape=[jax.ShapeDtypeStruct((t, d), F32),
                   jax.ShapeDtypeStruct((t, d), BF16),
                   jax.ShapeDtypeStruct((t, LANES), F32),
                   jax.ShapeDtypeStruct((SUBLANES, t), jnp.int32),
                   jax.ShapeDtypeStruct((1, LANES), F32)],
        scratch_shapes=[pltpu.VMEM((d // LANES, tm, LANES), F32), pltpu.VMEM((1, LANES), F32)],
        compiler_params=_params("arbitrary"),
        name="router",
    )(o, h, w_o, g, rw, rb, tri)


def _moe_kernel(te_ref, tv_ref, x_ref, wg_ref, wu_ref, wd_ref, *rest, tile0):
    out_ref, acc_ref = rest[-2:]
    i = tile0 + pl.program_id(0)
    f = pl.program_id(1)
    valid = tv_ref[i] > 0

    @pl.when(f == 0)
    def _():
        acc_ref[...] = jnp.zeros_like(acc_ref)

    @pl.when(valid)
    def _():
        acc_ref[...] += _swiglu_step(x_ref[...], wg_ref, wu_ref, wd_ref)

    @pl.when(f == pl.num_programs(1) - 1)
    def _():
        out_ref[...] = acc_ref[...].astype(out_ref.dtype)


def _moe_experts(xg, w_gu, w_d, tile_expert, tile_valid, j, tm, tile0, total_rows, prev):
    r, d = xg.shape
    ff = w_d.shape[2]
    tf = _ff_tile(ff)
    nf = ff // tf
    in_specs = [pl.BlockSpec((tm, d), lambda i, f, te, tv: (i, 0)),
                pl.BlockSpec((None, None, d, tf), lambda i, f, te, tv: (j, te[tile0 + i], 0, f)),
                pl.BlockSpec((None, None, d, tf), lambda i, f, te, tv: (j, te[tile0 + i], 0, nf + f)),
                pl.BlockSpec((None, None, tf, d), lambda i, f, te, tv: (j, te[tile0 + i], f, 0))]
    args = [tile_expert, tile_valid, xg, w_gu, w_gu, w_d]
    aliases = {}
    if prev is not None:
        in_specs.append(pl.BlockSpec(memory_space=pl.ANY))
        aliases = {len(args): 0}
        args.append(prev)
    grid_spec = pltpu.PrefetchScalarGridSpec(
        num_scalar_prefetch=2,
        grid=(r // tm, nf),
        in_specs=in_specs,
        out_specs=pl.BlockSpec((tm, d), lambda i, f, te, tv: (tile0 + i, 0)),
        scratch_shapes=[pltpu.VMEM((tm, d), F32)])
    return pl.pallas_call(
        functools.partial(_moe_kernel, tile0=tile0),
        grid_spec=grid_spec,
        out_shape=jax.ShapeDtypeStruct((total_rows, d), BF16),
        input_output_aliases=aliases,
        compiler_params=_params("parallel", "arbitrary"),
        name="moe_experts",
    )(*args)


def _dispatch_tables(route, cnt, tm):
    t = route.shape[1]
    npairs = t * TOP_K
    ntiles = npairs // tm + N_EXPERTS
    counts = cnt[0, :N_EXPERTS].astype(jnp.int32)
    padded = ((counts + tm - 1) // tm) * tm
    upto = jnp.tril(jnp.ones((N_EXPERTS, N_EXPERTS), bool))
    pend = jnp.sum(jnp.where(upto, padded[None, :], 0), axis=1)
    pstart = pend - padded
    cstart = jnp.sum(jnp.where(upto, counts[None, :], 0), axis=1) - counts
    experts, ranks = route[0:TOP_K], route[TOP_K:2 * TOP_K]
    pos = ranks
    for e in range(N_EXPERTS):
        pos = pos + jnp.where(experts == e, pstart[e], 0)
    order = jnp.argsort(pos.reshape(npairs))
    tile_start = jnp.arange(ntiles, dtype=jnp.int32) * tm
    tile_expert = jnp.minimum(jnp.searchsorted(pend, tile_start, side="right"),
                              N_EXPERTS - 1).astype(jnp.int32)
    tile_valid = (tile_start < pend[-1]).astype(jnp.int32)
    slot = jnp.arange(ntiles * tm, dtype=jnp.int32)
    slot_e = jnp.repeat(tile_expert, tm)
    within = slot - pstart[slot_e]
    src = jnp.clip(cstart[slot_e] + within, 0, npairs - 1)
    row_token = jnp.where(within < counts[slot_e], order[src] % t, slot % t).astype(jnp.int32)
    return row_token, pos, tile_expert, tile_valid


def _final_kernel(*refs, kind):
    *stream, g_ref, out_ref, slab_ref = refs
    x = stream[0][...] if kind == "rows" else _moe_combine(*stream)
    _batches_from_rows(_rms(x, g_ref[...]), slab_ref, out_ref)


def _final_norm(kind, stream, g, nb, tm):
    t, d = stream[0].shape
    return pl.pallas_call(
        functools.partial(_final_kernel, kind=kind),
        grid=(t // tm,),
        in_specs=_stream_specs(kind, stream, tm) + [pl.BlockSpec((1, d), lambda i: (0, 0))],
        out_specs=pl.BlockSpec((nb, tm // nb, d), lambda i: (0, i, 0)),
        out_shape=jax.ShapeDtypeStruct((nb, t // nb, d), F32),
        scratch_shapes=[pltpu.VMEM((d // LANES, tm, LANES), F32)],
        compiler_params=_params("parallel"),
        name="final_norm",
    )(*stream, g)


def kernel(x, mem, mem_norm_g, norm_mix_g, w_in, conv_w, conv_b, conv_ln_g, conv_ln_b, w_conv_pw,
           ssm_lambda_re, ssm_lambda_im, ssm_log_dt, ssm_b_re, ssm_b_im, ssm_c_re, ssm_c_im, ssm_d,
           w_ssm_glu, w_out, norm_xattn_g, w_q, w_kv, w_o, norm_ffn_g, ffn_w_gate_up, ffn_w_down,
           router_w, router_b, moe_w_gate_up, moe_w_down, final_norm_g):
    nb, s, d = x.shape
    assert nb == SUBLANES, "the row layout puts the batch on the sublane axis"
    depth = w_in.shape[0]
    cw = conv_w.shape[-1]
    sw = ssm_d.shape[1] * ssm_d.shape[2]
    t = nb * s
    tm = min(512, t)
    tm_ff = min(1024, t)
    ts = min(512, s)

    w_in_b = w_in.astype(BF16)
    w_pw_b = w_conv_pw.astype(BF16)
    w_glu_b = w_ssm_glu.astype(BF16)
    w_out_b = w_out.astype(BF16)
    w_q_b = w_q.astype(BF16)
    w_kv_b = w_kv.astype(BF16)
    w_o_b = w_o.astype(BF16)
    ffn_gu_b = ffn_w_gate_up.astype(BF16)
    ffn_d_b = ffn_w_down.astype(BF16)
    nslab = cw // LANES
    conv_w_rep = jnp.broadcast_to(conv_w.reshape(depth, CONV_KERNEL * nslab, 1, LANES),
                                  (depth, CONV_KERNEL * nslab, SUBLANES, LANES))
    conv_b_rep = jnp.broadcast_to(conv_b.reshape(depth, nslab, 1, LANES), (depth, nslab, SUBLANES, LANES))
    row3 = lambda a: a.reshape(a.shape[0], 1, a.shape[1])
    g_mix, g_x, g_ffn = row3(norm_mix_g), row3(norm_xattn_g), row3(norm_ffn_g)
    lg3, lb3 = row3(conv_ln_g), row3(conv_ln_b)
    rw_pad = jnp.pad(router_w, ((0, 0), (0, 0), (0, LANES - N_EXPERTS)))
    rw_hi = rw_pad.astype(BF16)
    rw_pad = jnp.concatenate([rw_hi, (rw_pad - rw_hi.astype(F32)).astype(BF16)], axis=-1)
    rb_pad = jnp.pad(router_b, ((0, 0), (0, LANES - N_EXPERTS)))
    tri = jnp.tril(jnp.ones((tm, tm), BF16))

    kv = _kv_proj(mem.reshape(nb * mem.shape[1], d), mem_norm_g.reshape(1, d), w_kv_b)
    kind, stream = "batches", (x,)

    for i in range(depth):
        tm_in = tm_ff if kind == "rows" else tm
        u_conv, u_ssm, gates, *rows = _in_proj(kind, stream, g_mix, w_in_b, i, cw, sw, tm_in)
        h = rows[0] if rows else stream[0]
        y_a = _conv_branch(u_conv, conv_w_rep, conv_b_rep, lg3, lb3, w_pw_b, i, tm)
        tables = _ssm_tables(ssm_lambda_re[i], ssm_lambda_im[i], ssm_log_dt[i], ssm_b_re[i], ssm_b_im[i],
                             ssm_c_re[i], ssm_c_im[i], ssm_d[i])
        y_s = _ssm_branch(u_ssm, *tables, rows=tm)
        h, q = _merge(y_s, w_glu_b, gates, y_a, h, w_out_b, g_x, w_q_b, i, nb, tm)
        o = _attention(q, kv, i, ts)
        j = i // 2
        if i % 2 == 0:
            h = _dense_ffn(o, h, w_o_b, g_ffn, ffn_gu_b, ffn_d_b, i, j, tm)
            kind, stream = "rows", (h,)
        else:
            h, hn, wts, route, cnt = _router(o, h, w_o_b, g_ffn, rw_pad[j], rb_pad[j:j + 1], tri, i, tm)
            row_token, pos, tile_expert, tile_valid = _dispatch_tables(route, cnt, tm_ff)
            rows_of = lambda a, ix: a.at[ix].get(mode="promise_in_bounds")
            ntiles = tile_expert.shape[0]
            bounds = sorted({0, ntiles // 9, ntiles // 3, (2 * ntiles) // 3, ntiles})
            yg = None
            for lo, hi in zip(bounds[:-1], bounds[1:]):
                xg = rows_of(hn, row_token[lo * tm_ff:hi * tm_ff])
                yg = _moe_experts(xg, moe_w_gate_up, moe_w_down, tile_expert, tile_valid, j, tm_ff,
                                  lo, ntiles * tm_ff, yg)
            kind, stream = "moe", (h, rows_of(yg, pos[0]), rows_of(yg, pos[1]), wts)

    return _final_norm(kind, stream, final_norm_g.reshape(1, d), nb, tm)
```

```python
import functools
import math

import jax
import jax.numpy as jnp
from jax import lax
from jax.experimental import pallas as pl
from jax.experimental.pallas import tpu as pltpu

F32 = jnp.float32
BF16 = jnp.bfloat16

RMS_EPS = 1e-6
LN_EPS = 1e-5
CONV_KERNEL = 31
SSM_GROUP = 16
SSM_STATE = 64
XATTN_HEADS = 4
N_EXPERTS = 8
TOP_K = 2

SUBLANES = 8
LANES = 128
HALO_BLOCK = 256
ROW_PART = 256
VMEM_LIMIT = 52 * 2 ** 20


def _params(*sem):
    return pltpu.CompilerParams(dimension_semantics=sem, vmem_limit_bytes=VMEM_LIMIT)


def _rms(x, g):
    return x * lax.rsqrt(jnp.mean(x * x, axis=-1, keepdims=True) + RMS_EPS) * g


def _dot(a, b):
    return jnp.dot(a, b, preferred_element_type=F32)


def _moe_combine(h_ref, y0_ref, y1_ref, wt_ref, rows=slice(None)):
    return h_ref[rows, :] + wt_ref[rows, 0:1] * y0_ref[rows, :] + wt_ref[rows, 1:2] * y1_ref[rows, :]


def _row_parts(tm, nsplit):
    sub = tm // nsplit
    return [slice(p * sub, (p + 1) * sub) for p in range(nsplit)]


def _in_kernel(*refs, kind, cw, sw, nsplit):
    if kind == "rows":
        h_ref, g_ref, w_ref, uc_ref, us_ref, gt_ref = refs
        x_of = lambda r: h_ref[r, :]
    elif kind == "batches":
        h_ref, g_ref, w_ref, uc_ref, us_ref, gt_ref, hrow_ref, slab_ref = refs
        hrow_ref[...] = _rows_from_batches(h_ref, slab_ref)
        x_of = lambda r: hrow_ref[r, :]
    else:
        h_ref, y0_ref, y1_ref, wt_ref, g_ref, w_ref, uc_ref, us_ref, gt_ref, hrow_ref = refs

        def x_of(r):
            x = _moe_combine(h_ref, y0_ref, y1_ref, wt_ref, r)
            hrow_ref[r, :] = x
            return x

    for r in _row_parts(uc_ref.shape[0], nsplit):
        xn = _rms(x_of(r), g_ref[...]).astype(BF16)
        a = _dot(xn, w_ref[:, 0:cw])
        gate = _dot(xn, w_ref[:, cw:2 * cw])
        uc_ref[r, :] = a * jax.nn.sigmoid(gate)
        us_ref[r, :] = _dot(xn, w_ref[:, 2 * cw:2 * cw + sw])
        gt_ref[r, :] = jax.nn.sigmoid(_dot(xn, w_ref[:, 2 * cw + sw:])).astype(BF16)


def _stream_specs(kind, stream, tm):
    d = stream[0].shape[-1]
    row = pl.BlockSpec((tm, d), lambda i: (i, 0))
    if kind == "rows":
        return [row]
    if kind == "batches":
        nb = stream[0].shape[0]
        return [pl.BlockSpec((nb, tm // nb, d), lambda i: (0, i, 0))]
    return [row, row, row, pl.BlockSpec((tm, LANES), lambda i: (i, 0))]


def _in_proj(kind, stream, g, w, layer, cw, sw, tm):
    d = stream[0].shape[-1]
    t = stream[0].size // d
    cols = w.shape[-1]
    ng = cols - 2 * cw - sw
    out_specs = [pl.BlockSpec((tm, cw), lambda i: (i, 0)),
                 pl.BlockSpec((tm, sw), lambda i: (i, 0)),
                 pl.BlockSpec((tm, ng), lambda i: (i, 0))]
    out_shape = [jax.ShapeDtypeStruct((t, cw), F32),
                 jax.ShapeDtypeStruct((t, sw), F32),
                 jax.ShapeDtypeStruct((t, ng), BF16)]
    scratch = []
    if kind != "rows":
        out_specs.append(pl.BlockSpec((tm, d), lambda i: (i, 0)))
        out_shape.append(jax.ShapeDtypeStruct((t, d), F32))
    if kind == "batches":
        scratch.append(pltpu.VMEM((d // LANES, tm, LANES), F32))
    return pl.pallas_call(
        functools.partial(_in_kernel, kind=kind, cw=cw, sw=sw, nsplit=tm // ROW_PART),
        grid=(t // tm,),
        in_specs=_stream_specs(kind, stream, tm) + [
            pl.BlockSpec((None, 1, d), lambda i: (layer, 0, 0)),
            pl.BlockSpec((None, d, cols), lambda i: (layer, 0, 0), pipeline_mode=pl.Buffered(1))],
        out_specs=out_specs,
        out_shape=out_shape,
        scratch_shapes=scratch,
        compiler_params=_params("parallel"),
        name="in_proj",
    )(*stream, g, w)


def _conv_kernel(cur_ref, prev_ref, cw_ref, cb_ref, lg_ref, lb_ref, wpw_ref, out_ref,
                 ext_ref, conv_ref, act_ref, *, rows_per_chunk):
    i = pl.program_id(0)
    tm, c = cur_ref.shape
    nslab = c // LANES
    prev = jnp.where(i > 0, prev_ref[...], 0.0)
    for j in range(nslab):
        ext_ref[j, 0:HALO_BLOCK, :] = prev[:, j * LANES:(j + 1) * LANES]
        ext_ref[j, HALO_BLOCK:, :] = cur_ref[:, j * LANES:(j + 1) * LANES]
    first = HALO_BLOCK - (CONV_KERNEL - 1) * SUBLANES
    rc = rows_per_chunk
    rcb = rc
    reps = rcb // SUBLANES

    def conv_block(n, carry):
        ci = n // nslab
        j = n % nslab
        r0 = pl.multiple_of(ci * rcb, rcb)
        accs = [cb_ref[j]] * reps
        for k in range(CONV_KERNEL):
            w = cw_ref[k * nslab + j]
            accs = [a + w * ext_ref[j, pl.ds(r0 + first + SUBLANES * (k + r), SUBLANES), :]
                    for r, a in enumerate(accs)]
        conv_ref[j, pl.ds(r0, rcb), :] = jnp.concatenate(accs, axis=0)
        return carry

    lax.fori_loop(0, (tm // rcb) * nslab, conv_block, 0, unroll=8)

    def norm_chunk(ci, carry):
        r0 = pl.multiple_of(ci * rc, rc)
        acc = jnp.concatenate([conv_ref[j, pl.ds(r0, rc), :] for j in range(nslab)], axis=-1)
        mu = jnp.mean(acc, axis=-1, keepdims=True)
        xc = acc - mu
        y = xc * lax.rsqrt(jnp.mean(xc * xc, axis=-1, keepdims=True) + LN_EPS)
        y = y * lg_ref[...] + lb_ref[...]
        act_ref[pl.ds(r0, rc), :] = jax.nn.silu(y).astype(BF16)
        return carry

    lax.fori_loop(0, tm // rc, norm_chunk, 0, unroll=8)
    out_ref[...] = _dot(act_ref[...], wpw_ref[...]).astype(BF16)


def _conv_branch(u, conv_w, conv_b, ln_g, ln_b, w_pw, layer, tm):
    t, c = u.shape
    d = w_pw.shape[-1]
    ratio = tm // HALO_BLOCK
    return pl.pallas_call(
        functools.partial(_conv_kernel, rows_per_chunk=64),
        grid=(t // tm,),
        in_specs=[pl.BlockSpec((tm, c), lambda i: (i, 0)),
                  pl.BlockSpec((HALO_BLOCK, c), lambda i: (jnp.maximum(i * ratio - 1, 0), 0)),
                  pl.BlockSpec((None,) + conv_w.shape[1:], lambda i: (layer, 0, 0, 0)),
                  pl.BlockSpec((None,) + conv_b.shape[1:], lambda i: (layer, 0, 0, 0)),
                  pl.BlockSpec((None, 1, c), lambda i: (layer, 0, 0)),
                  pl.BlockSpec((None, 1, c), lambda i: (layer, 0, 0)),
                  pl.BlockSpec((None, c, d), lambda i: (layer, 0, 0))],
        out_specs=pl.BlockSpec((tm, d), lambda i: (i, 0)),
        out_shape=jax.ShapeDtypeStruct((t, d), BF16),
        scratch_shapes=[pltpu.VMEM((c // LANES, tm + HALO_BLOCK, LANES), F32),
                        pltpu.VMEM((c // LANES, tm, LANES), F32), pltpu.VMEM((tm, c), BF16)],
        compiler_params=_params("parallel"),
        name="conv_branch",
    )(u, u, conv_w, conv_b, ln_g, ln_b, w_pw)


def _ssm_kernel(u_ref, bm_ref, cm_ref, a_ref, d_ref, out_ref, hs_ref, st_ref, *, nsplit):
    step_idx = pl.program_id(0)
    rows, width = u_ref.shape
    hw = width // 2
    hs_cols = hs_ref.shape[1] // 2
    hstates = hs_cols // 2
    sub = rows // nsplit

    @pl.when(step_idx == 0)
    def _():
        st_ref[...] = jnp.zeros_like(st_ref)

    cols = [(hf * hs_cols + j * LANES, hf * hs_cols + hstates + j * LANES, hf * hstates + j * LANES)
            for hf in range(2) for j in range(hstates // LANES)]

    def input_matmul(p):
        part = slice(p * sub, (p + 1) * sub)
        for hf in range(2):
            ub = u_ref[part, hf * hw:(hf + 1) * hw].astype(BF16)
            hs_ref[part, hf * hs_cols:(hf + 1) * hs_cols] = _dot(ub, bm_ref[hf])

    def scan(p):
        for cr, ci, ac in cols:
            ar = a_ref[0:SUBLANES, ac:ac + LANES]
            ai = a_ref[SUBLANES:2 * SUBLANES, ac:ac + LANES]
            hr = st_ref[:, cr:cr + LANES]
            hi = st_ref[:, ci:ci + LANES]
            for t in range(sub // SUBLANES):
                step = slice(p * sub + t * SUBLANES, p * sub + (t + 1) * SUBLANES)
                bur = hs_ref[step, cr:cr + LANES]
                bui = hs_ref[step, ci:ci + LANES]
                hr, hi = ar * hr - ai * hi + bur, ar * hi + ai * hr + bui
                hs_ref[step, cr:cr + LANES] = hr
                hs_ref[step, ci:ci + LANES] = hi
            st_ref[:, cr:cr + LANES] = hr
            st_ref[:, ci:ci + LANES] = hi

    def output_matmul(p):
        part = slice(p * sub, (p + 1) * sub)
        for hf in range(2):
            half = slice(hf * hw, (hf + 1) * hw)
            hb = hs_ref[part, hf * hs_cols:(hf + 1) * hs_cols].astype(BF16)
            y = _dot(hb, cm_ref[hf]) + d_ref[:, half] * u_ref[part, half]
            out_ref[part, half] = jax.nn.gelu(y).astype(BF16)

    input_matmul(0)
    for p in range(nsplit):
        if p + 1 < nsplit:
            input_matmul(p + 1)
        scan(p)
        output_matmul(p)


def _ssm_branch(u, bm, cm, a_rows, d_row, rows):
    t, width = u.shape
    ncols = 2 * bm.shape[-1]
    return pl.pallas_call(
        functools.partial(_ssm_kernel, nsplit=2),
        grid=(t // rows,),
        in_specs=[pl.BlockSpec((rows, width), lambda i: (i, 0)),
                  pl.BlockSpec(bm.shape, lambda i: (0, 0, 0)),
                  pl.BlockSpec(cm.shape, lambda i: (0, 0, 0)),
                  pl.BlockSpec(a_rows.shape, lambda i: (0, 0)),
                  pl.BlockSpec(d_row.shape, lambda i: (0, 0))],
        out_specs=pl.BlockSpec((rows, width), lambda i: (i, 0)),
        out_shape=jax.ShapeDtypeStruct((t, width), BF16),
        scratch_shapes=[pltpu.VMEM((rows, ncols), F32), pltpu.VMEM((SUBLANES, ncols), F32)],
        compiler_params=_params("arbitrary"),
        name="ssm_branch",
    )(u, bm, cm, a_rows, d_row)


def _ssm_tables(lam_re, lam_im, log_dt, b_re, b_im, c_re, c_im, d_skip):
    g, p = lam_re.shape
    gh = g // 2
    dt = jnp.exp(log_dt)[:, None]
    mag = jnp.exp(lam_re * dt)
    ar = mag * jnp.cos(lam_im * dt)
    ai = mag * jnp.sin(lam_im * dt)
    den = lam_re * lam_re + lam_im * lam_im
    xr = ar - 1.0
    kr = (xr * lam_re + ai * lam_im) / den
    ki = (ai * lam_re - xr * lam_im) / den
    bbar_r = kr[..., None] * b_re - ki[..., None] * b_im
    bbar_i = kr[..., None] * b_im + ki[..., None] * b_re
    eye = jnp.eye(gh, dtype=F32)

    def bd_in(m):
        m = m.reshape(2, gh, p, SSM_GROUP)
        return jnp.einsum("aqph,qr->aqhrp", m, eye).reshape(2, gh * SSM_GROUP, gh * p)

    def bd_out(m):
        m = m.reshape(2, gh, SSM_GROUP, p)
        return jnp.einsum("aqhp,qr->aqprh", m, eye).reshape(2, gh * p, gh * SSM_GROUP)

    bm = jnp.concatenate([bd_in(bbar_r), bd_in(bbar_i)], axis=-1).astype(BF16)
    cm = jnp.concatenate([bd_out(c_re), -bd_out(c_im)], axis=1).astype(BF16)
    a_rows = jnp.concatenate([jnp.broadcast_to(ar.reshape(1, g * p), (SUBLANES, g * p)),
                              jnp.broadcast_to(ai.reshape(1, g * p), (SUBLANES, g * p))], axis=0)
    return bm, cm, a_rows, d_skip.reshape(1, g * SSM_GROUP)


def _slabs_store(val, slab_ref, rows=slice(None)):
    for j in range(slab_ref.shape[0]):
        slab_ref[j, rows, :] = val[:, j * LANES:(j + 1) * LANES]


def _slabs_to_batches(slab_ref, out_ref):
    nb, n, d = out_ref.shape
    for b in range(nb):
        rows = [slab_ref[j, pl.ds(b, n, stride=nb), :] for j in range(d // LANES)]
        out_ref[b] = jnp.concatenate(rows, axis=-1).astype(out_ref.dtype)


def _batches_from_rows(val, slab_ref, out_ref):
    _slabs_store(val, slab_ref)
    _slabs_to_batches(slab_ref, out_ref)


def _batches_to_slabs(in_ref, slab_ref):
    nb, n, d = in_ref.shape
    for b in range(nb):
        xb = in_ref[b].astype(F32)
        for j in range(d // LANES):
            slab_ref[j, pl.ds(b, n, stride=nb), :] = xb[:, j * LANES:(j + 1) * LANES]


def _slab_rows(slab_ref, rows=slice(None)):
    return jnp.concatenate([slab_ref[j, rows, :] for j in range(slab_ref.shape[0])], axis=-1)


def _rows_from_batches(in_ref, slab_ref):
    _batches_to_slabs(in_ref, slab_ref)
    return _slab_rows(slab_ref)


def _merge_kernel(ys_ref, wglu_ref, gt_ref, ya_ref, h_ref, wout_ref, gx_ref, wq_ref, hout_ref, q_ref,
                  slab_ref, *, nsplit):
    tm, d = h_ref.shape
    for r in _row_parts(tm, nsplit):
        z = _dot(ys_ref[r, :], wglu_ref[...])
        yb = z[:, :d] * jax.nn.sigmoid(z[:, d:])
        m = gt_ref[r, :d].astype(F32) * ya_ref[r, :].astype(F32) + gt_ref[r, d:].astype(F32) * yb
        h2 = h_ref[r, :] + _dot(m.astype(BF16), wout_ref[...])
        hout_ref[r, :] = h2
        _slabs_store(_dot(_rms(h2, gx_ref[...]).astype(BF16), wq_ref[...]), slab_ref, r)
    _slabs_to_batches(slab_ref, q_ref)


def _merge(ys, w_glu, gates, ya, h, w_out, gx, w_q, layer, nb, tm):
    t, d = h.shape
    sw = ys.shape[1]
    n = tm // nb
    return pl.pallas_call(
        functools.partial(_merge_kernel, nsplit=1),
        grid=(t // tm,),
        in_specs=[pl.BlockSpec((tm, sw), lambda i: (i, 0)),
                  pl.BlockSpec((None, sw, 2 * d), lambda i: (layer, 0, 0)),
                  pl.BlockSpec((tm, 2 * d), lambda i: (i, 0)),
                  pl.BlockSpec((tm, d), lambda i: (i, 0)),
                  pl.BlockSpec((tm, d), lambda i: (i, 0)),
                  pl.BlockSpec((None, d, d), lambda i: (layer, 0, 0)),
                  pl.BlockSpec((None, 1, d), lambda i: (layer, 0, 0)),
                  pl.BlockSpec((None, d, d), lambda i: (layer, 0, 0))],
        out_specs=[pl.BlockSpec((tm, d), lambda i: (i, 0)),
                   pl.BlockSpec((nb, n, d), lambda i: (0, i, 0))],
        out_shape=[jax.ShapeDtypeStruct((t, d), F32), jax.ShapeDtypeStruct((nb, t // nb, d), BF16)],
        scratch_shapes=[pltpu.VMEM((d // LANES, tm, LANES), F32)],
        compiler_params=_params("parallel"),
        name="merge",
    )(ys, w_glu, gates, ya, h, w_out, gx, w_q)


def _kv_kernel(mem_ref, g_ref, w_ref, out_ref):
    xn = _rms(mem_ref[...], g_ref[...]).astype(BF16)
    out_ref[...] = _dot(xn, w_ref[...]).astype(BF16)


def _kv_proj(mem2d, g, w_kv):
    nl, d, d2 = w_kv.shape
    bm = mem2d.shape[0]
    return pl.pallas_call(
        _kv_kernel,
        grid=(nl, d2 // d),
        in_specs=[pl.BlockSpec((bm, d), lambda l, j: (0, 0)),
                  pl.BlockSpec((1, d), lambda l, j: (0, 0)),
                  pl.BlockSpec((None, d, d), lambda l, j: (l, 0, j))],
        out_specs=pl.BlockSpec((None, bm, d), lambda l, j: (l, 0, j)),
        out_shape=jax.ShapeDtypeStruct((nl, bm, d2), BF16),
        compiler_params=_params("parallel", "parallel"),
        name="kv_proj",
    )(mem2d, g, w_kv)


def _attn_kernel(q_ref, kv_ref, out_ref):
    d = q_ref.shape[1]
    hd = d // XATTN_HEADS
    scale = 1.0 / math.sqrt(hd)
    for n in range(XATTN_HEADS):
        qh = q_ref[:, n * hd:(n + 1) * hd]
        kh = kv_ref[:, n * hd:(n + 1) * hd]
        vh = kv_ref[:, d + n * hd:d + (n + 1) * hd]
        s = lax.dot_general(qh, kh, (((1,), (1,)), ((), ())), preferred_element_type=F32) * scale
        e = jnp.exp(s - jnp.max(s, axis=-1, keepdims=True))
        p = e / jnp.sum(e, axis=-1, keepdims=True)
        out_ref[:, n * hd:(n + 1) * hd] = _dot(p.astype(BF16), vh).astype(out_ref.dtype)


def _attention(q, kv, layer, ts):
    nb, s, d = q.shape
    m = kv.shape[1] // nb
    return pl.pallas_call(
        _attn_kernel,
        grid=(nb, s // ts),
        in_specs=[pl.BlockSpec((None, ts, d), lambda b, i: (b, i, 0)),
                  pl.BlockSpec((None, m, 2 * d), lambda b, i: (layer, b, 0))],
        out_specs=pl.BlockSpec((None, ts, d), lambda b, i: (b, i, 0)),
        out_shape=jax.ShapeDtypeStruct((nb, s, d), BF16),
        compiler_params=_params("parallel", "parallel"),
        name="xattn",
    )(q, kv)


def _attn_residual(h_ref, wo_ref, slab_ref, rows=slice(None)):
    return h_ref[rows, :] + _dot(_slab_rows(slab_ref, rows).astype(BF16), wo_ref[...])


def _swiglu_step(xn, wg_ref, wu_ref, wd_ref):
    g = _dot(xn, wg_ref[...].astype(BF16))
    u = _dot(xn, wu_ref[...].astype(BF16))
    return _dot((jax.nn.silu(g) * u).astype(BF16), wd_ref[...].astype(BF16))


def _ffn_kernel(o_ref, h_ref, wo_ref, g_ref, wgu_ref, wd_ref, out_ref, slab_ref, *, tf):
    ff = wd_ref.shape[0]
    _batches_to_slabs(o_ref, slab_ref)
    h2 = _attn_residual(h_ref, wo_ref, slab_ref)
    xn = _rms(h2, g_ref[...]).astype(BF16)
    out_ref[...] = h2
    for c in range(ff // tf):
        g = _dot(xn, wgu_ref[:, c * tf:(c + 1) * tf])
        u = _dot(xn, wgu_ref[:, ff + c * tf:ff + (c + 1) * tf])
        out_ref[...] += _dot((jax.nn.silu(g) * u).astype(BF16), wd_ref[c * tf:(c + 1) * tf, :])


def _ff_tile(ff):
    for tf in (512, 256, 128):
        if ff % tf == 0:
            return tf
    raise ValueError(f"hidden size {ff} is not a multiple of {LANES}")


def _dense_ffn(o, h, w_o, g, w_gu, w_d, layer, j, tm):
    t, d = h.shape
    nb = o.shape[0]
    n = tm // nb
    ff = w_d.shape[1]
    once = dict(pipeline_mode=pl.Buffered(1))
    return pl.pallas_call(
        functools.partial(_ffn_kernel, tf=_ff_tile(ff)),
        grid=(t // tm,),
        in_specs=[pl.BlockSpec((nb, n, d), lambda i: (0, i, 0)),
                  pl.BlockSpec((tm, d), lambda i: (i, 0)),
                  pl.BlockSpec((None, d, d), lambda i: (layer, 0, 0), **once),
                  pl.BlockSpec((None, 1, d), lambda i: (layer, 0, 0)),
                  pl.BlockSpec((None, d, 2 * ff), lambda i: (j, 0, 0), **once),
                  pl.BlockSpec((None, ff, d), lambda i: (j, 0, 0), **once)],
        out_specs=pl.BlockSpec((tm, d), lambda i: (i, 0)),
        out_shape=jax.ShapeDtypeStruct((t, d), F32),
        scratch_shapes=[pltpu.VMEM((d // LANES, tm, LANES), F32)],
        compiler_params=_params("parallel"),
        name="dense_ffn",
    )(o, h, w_o, g, w_gu, w_d)


def _router_kernel(o_ref, h_ref, wo_ref, g_ref, rw_ref, rb_ref, tri_ref, h2_ref, hn_ref, wt_ref, route_ref,
                   cnt_ref, slab_ref, base_ref):
    @pl.when(pl.program_id(0) == 0)
    def _():
        base_ref[...] = jnp.zeros_like(base_ref)

    _batches_to_slabs(o_ref, slab_ref)
    h2 = _attn_residual(h_ref, wo_ref, slab_ref)
    h2_ref[...] = h2
    xn = _rms(h2, g_ref[...])
    xh = xn.astype(BF16)
    hn_ref[...] = xh
    xl = (xn - xh.astype(F32)).astype(BF16)
    ph = _dot(xh, rw_ref[...])
    logits = ph[:, :LANES] + ph[:, LANES:] + _dot(xl, rw_ref[:, :LANES]) + rb_ref[...]
    lane = lax.broadcasted_iota(jnp.int32, logits.shape, 1)
    lanef = lane.astype(F32)
    neg = jnp.float32(-jnp.inf)
    big = jnp.float32(LANES)
    l1 = jnp.where(lane < N_EXPERTS, logits, neg)
    m1 = jnp.max(l1, axis=-1, keepdims=True)
    i1 = jnp.min(jnp.where(l1 == m1, lanef, big), axis=-1, keepdims=True)
    l2 = jnp.where(lanef == i1, neg, l1)
    m2 = jnp.max(l2, axis=-1, keepdims=True)
    i2 = jnp.min(jnp.where(l2 == m2, lanef, big), axis=-1, keepdims=True)
    e = jnp.exp(m2 - m1)
    w1 = 1.0 / (1.0 + e)
    w2 = e / (1.0 + e)
    top = jnp.float32(N_EXPERTS - 1)
    i1, i2 = jnp.minimum(i1, top), jnp.minimum(i2, top)
    wt_ref[...] = jnp.where(lane == 0, w1, jnp.where(lane == 1, w2, 0.0))
    hit1, hit2 = lanef == i1, lanef == i2
    seen = _dot(tri_ref[...], jnp.where(hit1 | hit2, 1.0, 0.0).astype(BF16)) + base_ref[...]
    r1 = jnp.sum(jnp.where(hit1, seen, 0.0), axis=-1, keepdims=True) - 1.0
    r2 = jnp.sum(jnp.where(hit2, seen, 0.0), axis=-1, keepdims=True) - 1.0
    base_ref[...] = seen[seen.shape[0] - 1:, :]
    cnt_ref[...] = base_ref[...]
    packed = jnp.where(lane == 0, i1, jnp.where(lane == 1, i2, jnp.where(lane == 2, r1,
                                                                         jnp.where(lane == 3, r2, 0.0))))
    route_ref[...] = packed.T[:SUBLANES, :].astype(jnp.int32)


def _router(o, h, w_o, g, rw, rb, tri, layer, tm):
    t, d = h.shape
    nb = o.shape[0]
    n = tm // nb
    return pl.pallas_call(
        _router_kernel,
        grid=(t // tm,),
        in_specs=[pl.BlockSpec((nb, n, d), lambda i: (0, i, 0)),
                  pl.BlockSpec((tm, d), lambda i: (i, 0)),
                  pl.BlockSpec((None, d, d), lambda i: (layer, 0, 0)),
                  pl.BlockSpec((None, 1, d), lambda i: (layer, 0, 0)),
                  pl.BlockSpec((d, 2 * LANES), lambda i: (0, 0)),
                  pl.BlockSpec((1, LANES), lambda i: (0, 0)),
                  pl.BlockSpec((tm, tm), lambda i: (0, 0))],
        out_specs=[pl.BlockSpec((tm, d), lambda i: (i, 0)),
                   pl.BlockSpec((tm, d), lambda i: (i, 0)),
                   pl.BlockSpec((tm, LANES), lambda i: (i, 0)),
                   pl.BlockSpec((SUBLANES, tm), lambda i: (0, i)),
                   pl.BlockSpec((1, LANES), lambda i: (0, 0))],
        out_shape=[jax.ShapeDtypeStruct((t, d), F32),
                   jax.ShapeDtypeStruct((t, d), BF16),
                   jax.ShapeDtypeStruct((t, LANES), F32),
                   jax.ShapeDtypeStruct((SUBLANES, t), jnp.int32),
                   jax.ShapeDtypeStruct((1, LANES), F32)],
        scratch_shapes=[pltpu.VMEM((d // LANES, tm, LANES), F32), pltpu.VMEM((1, LANES), F32)],
        compiler_params=_params("arbitrary"),
        name="router",
    )(o, h, w_o, g, rw, rb, tri)


def _moe_kernel(te_ref, tv_ref, x_ref, wg_ref, wu_ref, wd_ref, *rest, tile0):
    out_ref, acc_ref = rest[-2:]
    i = tile0 + pl.program_id(0)
    f = pl.program_id(1)
    valid = tv_ref[i] > 0

    @pl.when(f == 0)
    def _():
        acc_ref[...] = jnp.zeros_like(acc_ref)

    @pl.when(valid)
    def _():
        acc_ref[...] += _swiglu_step(x_ref[...], wg_ref, wu_ref, wd_ref)

    @pl.when(f == pl.num_programs(1) - 1)
    def _():
        out_ref[...] = acc_ref[...].astype(out_ref.dtype)


def _moe_experts(xg, w_gu, w_d, tile_expert, tile_valid, j, tm, tile0, total_rows, prev):
    r, d = xg.shape
    ff = w_d.shape[2]
    tf = _ff_tile(ff)
    nf = ff // tf
    in_specs = [pl.BlockSpec((tm, d), lambda i, f, te, tv: (i, 0)),
                pl.BlockSpec((None, None, d, tf), lambda i, f, te, tv: (j, te[tile0 + i], 0, f)),
                pl.BlockSpec((None, None, d, tf), lambda i, f, te, tv: (j, te[tile0 + i], 0, nf + f)),
                pl.BlockSpec((None, None, tf, d), lambda i, f, te, tv: (j, te[tile0 + i], f, 0))]
    args = [tile_expert, tile_valid, xg, w_gu, w_gu, w_d]
    aliases = {}
    if prev is not None:
        in_specs.append(pl.BlockSpec(memory_space=pl.ANY))
        aliases = {len(args): 0}
        args.append(prev)
    grid_spec = pltpu.PrefetchScalarGridSpec(
        num_scalar_prefetch=2,
        grid=(r // tm, nf),
        in_specs=in_specs,
        out_specs=pl.BlockSpec((tm, d), lambda i, f, te, tv: (tile0 + i, 0)),
        scratch_shapes=[pltpu.VMEM((tm, d), F32)])
    return pl.pallas_call(
        functools.partial(_moe_kernel, tile0=tile0),
        grid_spec=grid_spec,
        out_shape=jax.ShapeDtypeStruct((total_rows, d), BF16),
        input_output_aliases=aliases,
        compiler_params=_params("parallel", "arbitrary"),
        name="moe_experts",
    )(*args)


def _dispatch_tables(route, cnt, tm):
    t = route.shape[1]
    npairs = t * TOP_K
    ntiles = npairs // tm + N_EXPERTS
    counts = cnt[0, :N_EXPERTS].astype(jnp.int32)
    padded = ((counts + tm - 1) // tm) * tm
    upto = jnp.tril(jnp.ones((N_EXPERTS, N_EXPERTS), bool))
    pend = jnp.sum(jnp.where(upto, padded[None, :], 0), axis=1)
    pstart = pend - padded
    cstart = jnp.sum(jnp.where(upto, counts[None, :], 0), axis=1) - counts
    experts, ranks = route[0:TOP_K], route[TOP_K:2 * TOP_K]
    pos = ranks
    for e in range(N_EXPERTS):
        pos = pos + jnp.where(experts == e, pstart[e], 0)
    order = jnp.argsort(pos.reshape(npairs))
    tile_start = jnp.arange(ntiles, dtype=jnp.int32) * tm
    tile_expert = jnp.minimum(jnp.searchsorted(pend, tile_start, side="right"),
                              N_EXPERTS - 1).astype(jnp.int32)
    tile_valid = (tile_start < pend[-1]).astype(jnp.int32)
    slot = jnp.arange(ntiles * tm, dtype=jnp.int32)
    slot_e = jnp.repeat(tile_expert, tm)
    within = slot - pstart[slot_e]
    src = jnp.clip(cstart[slot_e] + within, 0, npairs - 1)
    row_token = jnp.where(within < counts[slot_e], order[src] % t, slot % t).astype(jnp.int32)
    return row_token, pos, tile_expert, tile_valid


def _final_kernel(*refs, kind):
    *stream, g_ref, out_ref, slab_ref = refs
    x = stream[0][...] if kind == "rows" else _moe_combine(*stream)
    _batches_from_rows(_rms(x, g_ref[...]), slab_ref, out_ref)


def _final_norm(kind, stream, g, nb, tm):
    t, d = stream[0].shape
    return pl.pallas_call(
        functools.partial(_final_kernel, kind=kind),
        grid=(t // tm,),
        in_specs=_stream_specs(kind, stream, tm) + [pl.BlockSpec((1, d), lambda i: (0, 0))],
        out_specs=pl.BlockSpec((nb, tm // nb, d), lambda i: (0, i, 0)),
        out_shape=jax.ShapeDtypeStruct((nb, t // nb, d), F32),
        scratch_shapes=[pltpu.VMEM((d // LANES, tm, LANES), F32)],
        compiler_params=_params("parallel"),
        name="final_norm",
    )(*stream, g)


def kernel(x, mem, mem_norm_g, norm_mix_g, w_in, conv_w, conv_b, conv_ln_g, conv_ln_b, w_conv_pw,
           ssm_lambda_re, ssm_lambda_im, ssm_log_dt, ssm_b_re, ssm_b_im, ssm_c_re, ssm_c_im, ssm_d,
           w_ssm_glu, w_out, norm_xattn_g, w_q, w_kv, w_o, norm_ffn_g, ffn_w_gate_up, ffn_w_down,
           router_w, router_b, moe_w_gate_up, moe_w_down, final_norm_g):
    nb, s, d = x.shape
    assert nb == SUBLANES, "the row layout puts the batch on the sublane axis"
    depth = w_in.shape[0]
    cw = conv_w.shape[-1]
    sw = ssm_d.shape[1] * ssm_d.shape[2]
    t = nb * s
    tm = min(512, t)
    tm_ff = min(1024, t)
    ts = min(512, s)

    w_in_b = w_in.astype(BF16)
    w_pw_b = w_conv_pw.astype(BF16)
    w_glu_b = w_ssm_glu.astype(BF16)
    w_out_b = w_out.astype(BF16)
    w_q_b = w_q.astype(BF16)
    w_kv_b = w_kv.astype(BF16)
    w_o_b = w_o.astype(BF16)
    ffn_gu_b = ffn_w_gate_up.astype(BF16)
    ffn_d_b = ffn_w_down.astype(BF16)
    nslab = cw // LANES
    conv_w_rep = jnp.broadcast_to(conv_w.reshape(depth, CONV_KERNEL * nslab, 1, LANES),
                                  (depth, CONV_KERNEL * nslab, SUBLANES, LANES))
    conv_b_rep = jnp.broadcast_to(conv_b.reshape(depth, nslab, 1, LANES), (depth, nslab, SUBLANES, LANES))
    row3 = lambda a: a.reshape(a.shape[0], 1, a.shape[1])
    g_mix, g_x, g_ffn = row3(norm_mix_g), row3(norm_xattn_g), row3(norm_ffn_g)
    lg3, lb3 = row3(conv_ln_g), row3(conv_ln_b)
    rw_pad = jnp.pad(router_w, ((0, 0), (0, 0), (0, LANES - N_EXPERTS)))
    rw_hi = rw_pad.astype(BF16)
    rw_pad = jnp.concatenate([rw_hi, (rw_pad - rw_hi.astype(F32)).astype(BF16)], axis=-1)
    rb_pad = jnp.pad(router_b, ((0, 0), (0, LANES - N_EXPERTS)))
    tri = jnp.tril(jnp.ones((tm, tm), BF16))

    kv = _kv_proj(mem.reshape(nb * mem.shape[1], d), mem_norm_g.reshape(1, d), w_kv_b)
    kind, stream = "batches", (x,)

    for i in range(depth):
        tm_in = tm_ff if kind == "rows" else tm
        u_conv, u_ssm, gates, *rows = _in_proj(kind, stream, g_mix, w_in_b, i, cw, sw, tm_in)
        h = rows[0] if rows else stream[0]
        y_a = _conv_branch(u_conv, conv_w_rep, conv_b_rep, lg3, lb3, w_pw_b, i, tm)
        tables = _ssm_tables(ssm_lambda_re[i], ssm_lambda_im[i], ssm_log_dt[i], ssm_b_re[i], ssm_b_im[i],
                             ssm_c_re[i], ssm_c_im[i], ssm_d[i])
        y_s = _ssm_branch(u_ssm, *tables, rows=tm)
        h, q = _merge(y_s, w_glu_b, gates, y_a, h, w_out_b, g_x, w_q_b, i, nb, tm)
        o = _attention(q, kv, i, ts)
        j = i // 2
        if i % 2 == 0:
            h = _dense_ffn(o, h, w_o_b, g_ffn, ffn_gu_b, ffn_d_b, i, j, tm)
            kind, stream = "rows", (h,)
        else:
            h, hn, wts, route, cnt = _router(o, h, w_o_b, g_ffn, rw_pad[j], rb_pad[j:j + 1], tri, i, tm)
            row_token, pos, tile_expert, tile_valid = _dispatch_tables(route, cnt, tm_ff)
            rows_of = lambda a, ix: a.at[ix].get(mode="promise_in_bounds")
            ntiles = tile_expert.shape[0]
            bounds = sorted({0, ntiles // 9, ntiles // 3, (2 * ntiles) // 3, ntiles})
            yg = None
            for lo, hi in zip(bounds[:-1], bounds[1:]):
                xg = rows_of(hn, row_token[lo * tm_ff:hi * tm_ff])
                yg = _moe_experts(xg, moe_w_gate_up, moe_w_down, tile_expert, tile_valid, j, tm_ff,
                                  lo, ntiles * tm_ff, yg)
            kind, stream = "moe", (h, rows_of(yg, pos[0]), rows_of(yg, pos[1]), wts)

    return _final_norm(kind, stream, final_norm_g.reshape(1, d), nb, tm)
```
